```python
import jax, jax.numpy as jnp
from jax import lax
import numpy as np

D_MODEL = 4096
BATCH = 4
SEQ = 4096
DEPTH = 2

BLOCK = 128
RMS_EPS = 1e-6
MLA_HEADS = 16
Q_LORA = 1536
KV_LORA = 512
NOPE_DIM = 128
ROPE_DIM = 64
MLA_V_DIM = 128
ROPE_THETA = 10000.0
FOX_HEADS = 16
FOX_HEAD_DIM = 128
SWA_Q_HEADS = 32
SWA_KV_HEADS = 4
SWA_HEAD_DIM = 64
WINDOW = 128
N_BRANCHES = 3
MLA_WIDTH = MLA_HEADS * MLA_V_DIM
FOX_WIDTH = FOX_HEADS * FOX_HEAD_DIM
SWA_WIDTH = SWA_Q_HEADS * SWA_HEAD_DIM
SWA_KV_WIDTH = SWA_KV_HEADS * SWA_HEAD_DIM
MIX_WIDTH = MLA_WIDTH + FOX_WIDTH + SWA_WIDTH
IN_SIZES = (Q_LORA, KV_LORA, ROPE_DIM,
            FOX_WIDTH, FOX_WIDTH, FOX_WIDTH, FOX_HEADS,
            SWA_WIDTH, SWA_KV_WIDTH, SWA_KV_WIDTH,
            N_BRANCHES * D_MODEL)
IN_WIDTH = sum(IN_SIZES)
D_FF_DENSE = 14336
N_EXPERTS = 8
TOP_K = 2
D_FF_EXPERT = 4096
N_DENSE_LAYERS = (DEPTH + 1) // 2
N_MOE_LAYERS = DEPTH // 2

kernel_name = 'hybrid_mla_fox_swa_gated_moe'


def _split_points(sizes):
    pts, acc = [], 0
    for s in sizes[:-1]:
        acc += s
        pts.append(acc)
    return pts


def rms_norm(x, g):
    x32 = x.astype(jnp.float32)
    y = x32 * lax.rsqrt(jnp.mean(x32 * x32, axis=-1, keepdims=True) + RMS_EPS)
    return (y * g.astype(jnp.float32)).astype(x.dtype)


def rope(x, pos):
    half = x.shape[-1] // 2
    inv = ROPE_THETA ** (-jnp.arange(half, dtype=jnp.float32) / half)
    ang = pos.astype(jnp.float32)[:, None] * inv[None, :]
    cos = jnp.cos(ang)[:, None, :]
    sin = jnp.sin(ang)[:, None, :]
    x32 = x.astype(jnp.float32)
    x1, x2 = x32[..., :half], x32[..., half:]
    return jnp.concatenate([x1 * cos - x2 * sin, x2 * cos + x1 * sin], axis=-1).astype(x.dtype)


def alibi_slopes(n):
    return jnp.exp2(-8.0 * jnp.arange(1, n + 1, dtype=jnp.float32) / n)


def causal_block_attention(q, k, v, scale, cum=None):
    B, S, H, dk = q.shape
    dv = v.shape[-1]
    nb = S // BLOCK
    qb = q.reshape(B, nb, BLOCK, H, dk).transpose(1, 0, 2, 3, 4)
    key_pos = jnp.arange(S)
    cum_k = None if cum is None else cum.transpose(0, 2, 1)

    def attend(i, q_blk, cum_blk):
        s = jnp.einsum('bqhd,bkhd->bhqk', q_blk, k, preferred_element_type=jnp.float32) * scale
        if cum_blk is not None:
            s = s + cum_blk[..., None] - cum_k[:, :, None, :]
        q_pos = i * BLOCK + jnp.arange(BLOCK)
        mask = q_pos[:, None] >= key_pos[None, :]
        p = jax.nn.softmax(jnp.where(mask, s, -jnp.inf), axis=-1)
        return jnp.einsum('bhqk,bkhd->bqhd', p.astype(v.dtype), v)

    idx = jnp.arange(nb)
    if cum is None:
        out = lax.map(lambda a: attend(a[0], a[1], None), (idx, qb))
    else:
        cb = cum.reshape(B, nb, BLOCK, H).transpose(1, 0, 3, 2)
        out = lax.map(lambda a: attend(a[0], a[1], a[2]), (idx, qb, cb))
    return out.transpose(1, 0, 2, 3, 4).reshape(B, S, H * dv)


def sliding_window_sink_attention(q, k, v, sinks, slopes):
    B, S, Hkv, G, d = q.shape
    nb = S // BLOCK
    qb = q.reshape(B, nb, BLOCK, Hkv, G, d)

    def band(t):
        prev = jnp.pad(t, ((0, 0), (BLOCK, 0), (0, 0), (0, 0)))[:, :S]
        return jnp.concatenate([prev.reshape(B, nb, BLOCK, Hkv, d),
                                t.reshape(B, nb, BLOCK, Hkv, d)], axis=2)

    kb, vb = band(k), band(v)
    s = jnp.einsum('bnqhgd,bnkhd->bnhgqk', qb, kb, preferred_element_type=jnp.float32) * (d ** -0.5)
    dist = jnp.arange(BLOCK)[:, None] + BLOCK - jnp.arange(2 * BLOCK)[None, :]
    key_pos = (jnp.arange(nb) * BLOCK - BLOCK)[:, None] + jnp.arange(2 * BLOCK)[None, :]
    valid = (dist >= 0)[None] & (dist < WINDOW)[None] & (key_pos >= 0)[:, None, :]
    alibi = -slopes.astype(jnp.float32).reshape(Hkv, G)[:, :, None, None] * dist.astype(jnp.float32)
    s = jnp.where(valid[None, :, None, None], s + alibi, -jnp.inf)
    sink = jnp.broadcast_to(sinks.astype(jnp.float32).reshape(1, 1, Hkv, G, 1, 1), s.shape[:-1] + (1,))
    p = jax.nn.softmax(jnp.concatenate([s, sink], axis=-1), axis=-1)[..., :-1]
    o = jnp.einsum('bnhgqk,bnkhd->bnqhgd', p.astype(v.dtype), vb)
    return o.reshape(B, S, Hkv * G * d)


def hybrid_mixer(xn, w_in, b_forget, b_gate, g_q, g_kv, w_uq, w_ukv, sinks, w_branch, w_out):
    B, S, _ = xn.shape
    pos = jnp.arange(S)
    h = xn @ w_in
    (c_q, c_kv, k_r, fq, fk, fv, f_logit, sq, sk, sv, gate_logit) = jnp.split(
        h, _split_points(IN_SIZES), axis=-1)

    q = (rms_norm(c_q, g_q) @ w_uq).reshape(B, S, MLA_HEADS, NOPE_DIM + ROPE_DIM)
    q = jnp.concatenate([q[..., :NOPE_DIM], rope(q[..., NOPE_DIM:], pos)], axis=-1)
    kv = (rms_norm(c_kv, g_kv) @ w_ukv).reshape(B, S, MLA_HEADS, NOPE_DIM + MLA_V_DIM)
    k_rope = jnp.broadcast_to(rope(k_r[:, :, None, :], pos), (B, S, MLA_HEADS, ROPE_DIM))
    k = jnp.concatenate([kv[..., :NOPE_DIM], k_rope], axis=-1)
    o_mla = causal_block_attention(q, k, kv[..., NOPE_DIM:], (NOPE_DIM + ROPE_DIM) ** -0.5)

    log_f = jax.nn.log_sigmoid((f_logit + b_forget).astype(jnp.float32))
    cum = jnp.cumsum(log_f, axis=1)
    fox_shape = (B, S, FOX_HEADS, FOX_HEAD_DIM)
    o_fox = causal_block_attention(fq.reshape(fox_shape), fk.reshape(fox_shape), fv.reshape(fox_shape),
                                   FOX_HEAD_DIM ** -0.5, cum)

    g = SWA_Q_HEADS // SWA_KV_HEADS
    o_swa = sliding_window_sink_attention(
        sq.reshape(B, S, SWA_KV_HEADS, g, SWA_HEAD_DIM),
        sk.reshape(B, S, SWA_KV_HEADS, SWA_HEAD_DIM),
        sv.reshape(B, S, SWA_KV_HEADS, SWA_HEAD_DIM),
        sinks, alibi_slopes(SWA_Q_HEADS))

    y_mla = o_mla @ w_branch[:MLA_WIDTH]
    y_fox = o_fox @ w_branch[MLA_WIDTH:MLA_WIDTH + FOX_WIDTH]
    y_swa = o_swa @ w_branch[MLA_WIDTH + FOX_WIDTH:]
    gt_mla, gt_fox, gt_swa = jnp.split(jax.nn.sigmoid(gate_logit + b_gate), N_BRANCHES, axis=-1)
    return (gt_mla * y_mla + gt_fox * y_fox + gt_swa * y_swa) @ w_out


def swiglu(x, w_gate, w_up, w_down):
    return (jax.nn.silu(x @ w_gate) * (x @ w_up)) @ w_down


def moe_swiglu(xn, w_router, w_gate, w_up, w_down):
    logits = jnp.einsum('bsd,de->bse', xn, w_router, preferred_element_type=jnp.float32)
    top_v, top_i = lax.top_k(logits, TOP_K)
    weights = jax.nn.softmax(top_v, axis=-1)
    combine = jnp.einsum('bsk,bske->bse', weights,
                         jax.nn.one_hot(top_i, N_EXPERTS, dtype=jnp.float32))
    out = jnp.zeros_like(xn)
    for e in range(N_EXPERTS):
        out = out + combine[..., e:e + 1].astype(xn.dtype) * swiglu(xn, w_gate[e], w_up[e], w_down[e])
    return out


def _normal(key, shape, scale):
    return jax.random.normal(key, shape, jnp.float32) * scale


def setup_inputs(seed: int = 0) -> dict:
    key = jax.random.key(seed)
    ks = jax.random.split(key, 24)
    L, D = DEPTH, D_MODEL
    w_branch = jnp.concatenate([
        _normal(ks[9], (L, MLA_WIDTH, D), MLA_WIDTH ** -0.5),
        _normal(ks[10], (L, FOX_WIDTH, D), FOX_WIDTH ** -0.5),
        _normal(ks[11], (L, SWA_WIDTH, D), SWA_WIDTH ** -0.5)], axis=1)
    return {
        'x': jax.random.normal(ks[0], (BATCH, SEQ, D), jnp.float32),
        'g_mix_norm': 1.0 + _normal(ks[1], (L, D), 0.1),
        'w_in': _normal(ks[2], (L, D, IN_WIDTH), D ** -0.5),
        'b_forget': jax.random.uniform(ks[3], (L, FOX_HEADS), jnp.float32, 1.0, 4.0),
        'b_gate': _normal(ks[4], (L, N_BRANCHES * D), 0.1),
        'g_q_norm': 1.0 + _normal(ks[5], (L, Q_LORA), 0.1),
        'g_kv_norm': 1.0 + _normal(ks[6], (L, KV_LORA), 0.1),
        'w_uq': _normal(ks[7], (L, Q_LORA, MLA_HEADS * (NOPE_DIM + ROPE_DIM)), Q_LORA ** -0.5),
        'w_ukv': _normal(ks[8], (L, KV_LORA, MLA_HEADS * (NOPE_DIM + MLA_V_DIM)), KV_LORA ** -0.5),
        'sinks': _normal(ks[12], (L, SWA_Q_HEADS), 1.0),
        'w_branch': w_branch,
        'w_out': _normal(ks[13], (L, D, D), D ** -0.5),
        'g_ffn_norm': 1.0 + _normal(ks[14], (L, D), 0.1),
        'w_dense_gate': _normal(ks[15], (N_DENSE_LAYERS, D, D_FF_DENSE), D ** -0.5),
        'w_dense_up': _normal(ks[16], (N_DENSE_LAYERS, D, D_FF_DENSE), D ** -0.5),
        'w_dense_down': _normal(ks[17], (N_DENSE_LAYERS, D_FF_DENSE, D), D_FF_DENSE ** -0.5),
        'w_router': _normal(ks[18], (N_MOE_LAYERS, D, N_EXPERTS), D ** -0.5),
        'w_exp_gate': _normal(ks[19], (N_MOE_LAYERS, N_EXPERTS, D, D_FF_EXPERT), D ** -0.5),
        'w_exp_up': _normal(ks[20], (N_MOE_LAYERS, N_EXPERTS, D, D_FF_EXPERT), D ** -0.5),
        'w_exp_down': _normal(ks[21], (N_MOE_LAYERS, N_EXPERTS, D_FF_EXPERT, D), D_FF_EXPERT ** -0.5),
        'g_final': 1.0 + _normal(ks[22], (D,), 0.1),
    }


def reference(x, g_mix_norm, w_in, b_forget, b_gate, g_q_norm, g_kv_norm, w_uq, w_ukv, sinks,
              w_branch, w_out, g_ffn_norm, w_dense_gate, w_dense_up, w_dense_down,
              w_router, w_exp_gate, w_exp_up, w_exp_down, g_final):
    for l in range(DEPTH):
        xn = rms_norm(x, g_mix_norm[l])
        x = x + hybrid_mixer(xn, w_in[l], b_forget[l], b_gate[l], g_q_norm[l], g_kv_norm[l],
                             w_uq[l], w_ukv[l], sinks[l], w_branch[l], w_out[l])
        xn = rms_norm(x, g_ffn_norm[l])
        j = l // 2
        if l % 2 == 0:
            x = x + swiglu(xn, w_dense_gate[j], w_dense_up[j], w_dense_down[j])
        else:
            x = x + moe_swiglu(xn, w_router[j], w_exp_gate[j], w_exp_up[j], w_exp_down[j])
    return rms_norm(x, g_final)
```

```python
import functools

import jax
import jax.numpy as jnp
from jax import lax
from jax.experimental import pallas as pl
from jax.experimental.pallas import tpu as pltpu

RMS_EPS = 1e-6
MLA_HEADS = 16
Q_LORA = 1536
KV_LORA = 512
NOPE_DIM = 128
ROPE_DIM = 64
MLA_V_DIM = 128
ROPE_THETA = 10000.0
FOX_HEADS = 16
FOX_HEAD_DIM = 128
SWA_Q_HEADS = 32
SWA_KV_HEADS = 4
SWA_HEAD_DIM = 64
WINDOW = 128
N_BRANCHES = 3
N_EXPERTS = 8
TOP_K = 2

LANE = 128
V7X_VMEM_BYTES = 64 * 1024 * 1024
V7X_VMEM_BUDGET = 56 * 1024 * 1024
COMPILER_SCRATCH_BYTES = 12 * 1024 * 1024

MM_BM = 1024
MM_BN = 1024
MM_BK = 2048
MM_FULL_K = 4096
MM_BN_MULTI = 512
NORM_BM = 512
FLASH_BQ = 512
FLASH_BK = 512
CUM_BLOCK = 512
MLA_KV_BM = 512
ROUTER_BM = 512

MLA_QK = NOPE_DIM + 2 * ROPE_DIM


def _params(semantics, vmem_bytes):
    limit = min(V7X_VMEM_BUDGET, vmem_bytes + COMPILER_SCRATCH_BYTES)
    return pltpu.CompilerParams(dimension_semantics=semantics, vmem_limit_bytes=int(limit))


def _nbytes(shape, dtype):
    n = 1
    for s in shape:
        n *= s
    return n * jnp.dtype(dtype).itemsize


def _tile(full, want):
    t = min(full, want)
    while full % t:
        t //= 2
    return t


def _rmsnorm_body(x_ref, g_ref, o_ref):
    x = x_ref[...].astype(jnp.float32)
    y = x * lax.rsqrt(jnp.mean(x * x, axis=-1, keepdims=True) + RMS_EPS)
    o_ref[...] = (y * g_ref[...]).astype(o_ref.dtype)


def _rmsnorm(x, g, out_dtype, *, col_block=0, width=None, name="rmsnorm"):
    rows = x.shape[0]
    width = x.shape[1] if width is None else width
    bm = _tile(rows, NORM_BM)
    vmem = 2 * _nbytes((bm, width), x.dtype) + 2 * _nbytes((bm, width), out_dtype)
    return pl.pallas_call(
        _rmsnorm_body,
        grid=(rows // bm,),
        in_specs=[pl.BlockSpec((bm, width), lambda i: (i, col_block)),
                  pl.BlockSpec((1, width), lambda i: (0, 0))],
        out_specs=pl.BlockSpec((bm, width), lambda i: (i, 0)),
        out_shape=jax.ShapeDtypeStruct((rows, width), out_dtype),
        compiler_params=_params(("parallel",), vmem),
        name=name,
    )(x, g.reshape(1, width).astype(jnp.float32))


def _mm_body(n_a, n_w, n_x, dots, nk, epilogue, *refs):
    a = refs[:n_a]
    w = refs[n_a:n_a + n_w]
    xs = refs[n_a + n_w:n_a + n_w + n_x]
    o = refs[n_a + n_w + n_x]
    accs = refs[n_a + n_w + n_x + 1:]

    def products():
        return [jnp.dot(a[ai][...], w[wi][...], preferred_element_type=jnp.float32)
                for ai, wi in dots]

    def finish(vals):
        o[...] = epilogue(vals, *[x[...] for x in xs]).astype(o.dtype)

    if nk == 1:
        finish(products())
        return

    k = pl.program_id(2)

    @pl.when(k == 0)
    def _():
        for acc, p in zip(accs, products()):
            acc[...] = p

    @pl.when((k > 0) & (k < nk - 1))
    def _():
        for acc, p in zip(accs, products()):
            acc[...] += p

    @pl.when(k == nk - 1)
    def _():
        finish([acc[...] + p for acc, p in zip(accs, products())])


def _mm(name, a_ops, w_ops, x_ops, dots, epilogue, *, m, n, bm, bn, nk, out_dtype):
    ops = list(a_ops) + list(w_ops) + list(x_ops)
    vmem = 2 * _nbytes((bm, bn), out_dtype)
    for arr, spec in ops:
        blk = [d for d in spec.block_shape if d is not None]
        vmem += 2 * _nbytes(blk, arr.dtype)
    scratch = []
    if nk > 1:
        scratch = [pltpu.VMEM((bm, bn), jnp.float32) for _ in dots]
        vmem += len(dots) * _nbytes((bm, bn), jnp.float32)
    body = functools.partial(_mm_body, len(a_ops), len(w_ops), len(x_ops), tuple(dots), nk, epilogue)
    return pl.pallas_call(
        body,
        grid=(m // bm, n // bn, nk),
        in_specs=[spec for _, spec in ops],
        out_specs=pl.BlockSpec((bm, bn), lambda i, j, k: (i, j)),
        out_shape=jax.ShapeDtypeStruct((m, n), out_dtype),
        scratch_shapes=scratch,
        compiler_params=_params(("parallel", "parallel", "arbitrary"), vmem),
        name=name,
    )(*[arr for arr, _ in ops])


def _a_spec(bm, bk):
    return pl.BlockSpec((bm, bk), lambda i, j, k: (i, k))


def _w_spec(bk, bn):
    return pl.BlockSpec((bk, bn), lambda i, j, k: (k, j))


def _tile_spec(bm, bn, col_off=0):
    return pl.BlockSpec((bm, bn), lambda i, j, k: (i, j + col_off))


def _row_spec(bn, col_off=0):
    return pl.BlockSpec((1, bn), lambda i, j, k: (0, j + col_off))


def _ep_plain(vals):
    return vals[0]


def _ep_residual(vals, res):
    return res + vals[0]


def _ep_sigmoid_bias(vals, bias):
    return jax.nn.sigmoid(vals[0] + bias)


def _ep_swiglu(vals):
    return jax.nn.silu(vals[0]) * vals[1]


def _ep_merge(vals, g0, g1, g2):
    return (g0.astype(jnp.float32) * vals[0] + g1.astype(jnp.float32) * vals[1]
            + g2.astype(jnp.float32) * vals[2])


def _linear(name, a, w, out_dtype, *, epilogue=_ep_plain, extras=(), bn=MM_BN):
    m, kdim = a.shape
    n = w.shape[1]
    bm, bn, bk = _tile(m, MM_BM), _tile(n, bn), _tile(kdim, MM_BK if kdim > MM_FULL_K else kdim)
    x_ops = []
    for arr, kind in extras:
        x_ops.append((arr, _tile_spec(bm, bn) if kind == "tile" else _row_spec(bn)))
    return _mm(name, [(a, _a_spec(bm, bk))], [(w, _w_spec(bk, bn))], x_ops, [(0, 0)], epilogue,
               m=m, n=n, bm=bm, bn=bn, nk=kdim // bk, out_dtype=out_dtype)


def _rope_tile(t, cos_t, sin_a, sin_b):
    half = ROPE_DIM // 2
    return (t * cos_t + pltpu.roll(t, LANE - half, axis=1) * sin_a
            + pltpu.roll(t, half, axis=1) * sin_b)


def _rope_tables(seq):
    half = ROPE_DIM // 2
    inv = ROPE_THETA ** (-jnp.arange(half, dtype=jnp.float32) / half)
    ang = jnp.arange(seq, dtype=jnp.float32)[:, None] * inv[None, :]
    cos, sin = jnp.cos(ang), jnp.sin(ang)
    z = jnp.zeros_like(cos)
    pad = jnp.zeros((seq, LANE - 2 * half), jnp.float32)
    cos_t = jnp.concatenate([cos, cos, pad], axis=1)
    sin_a = jnp.concatenate([-sin, z, pad], axis=1)
    sin_b = jnp.concatenate([z, sin, pad], axis=1)
    return cos_t, sin_a, sin_b


def _ep_mla_q(vals, cos_t, sin_a, sin_b):
    acc = vals[0]
    pieces = []
    for h in range(acc.shape[1] // MLA_QK):
        lo = h * MLA_QK
        pieces.append(acc[:, lo:lo + NOPE_DIM])
        pieces.append(_rope_tile(acc[:, lo + NOPE_DIM:lo + MLA_QK], cos_t, sin_a, sin_b))
    return jnp.concatenate(pieces, axis=1)


def _mla_q(cq_n, w_uq_p, tables, seq):
    m, kdim = cq_n.shape
    n = w_uq_p.shape[1]
    bm = _tile(seq, MM_BM)
    bn = _tile(n, MM_BN)
    nseq = seq // bm
    tab_spec = pl.BlockSpec((bm, LANE), lambda i, j, k: (i % nseq, 0))
    return _mm("mla_q", [(cq_n, _a_spec(bm, kdim))], [(w_uq_p, _w_spec(kdim, bn))],
               [(t, tab_spec) for t in tables], [(0, 0)], _ep_mla_q,
               m=m, n=n, bm=bm, bn=bn, nk=1, out_dtype=jnp.bfloat16)


def _mla_kv_body(heads, a_ref, wk_ref, wv_ref, kr_ref, cos_ref, sina_ref, sinb_ref, k_ref, v_ref):
    a = a_ref[...]
    kn = jnp.dot(a, wk_ref[...], preferred_element_type=jnp.float32)
    v_ref[...] = jnp.dot(a, wv_ref[...], preferred_element_type=jnp.float32).astype(v_ref.dtype)
    kr = _rope_tile(kr_ref[...], cos_ref[...], sina_ref[...], sinb_ref[...]).astype(k_ref.dtype)
    for h in range(heads):
        k_ref[:, h * MLA_QK:h * MLA_QK + NOPE_DIM] = kn[:, h * NOPE_DIM:(h + 1) * NOPE_DIM].astype(k_ref.dtype)
        k_ref[:, h * MLA_QK + NOPE_DIM:(h + 1) * MLA_QK] = kr


def _mla_kv(ckv_n, w_k, w_v, lat, kr_col_block, tables, seq):
    m, kdim = ckv_n.shape
    heads = w_k.shape[1] // NOPE_DIM
    bm = _tile(seq, MLA_KV_BM)
    nseq = seq // bm
    tab_spec = pl.BlockSpec((bm, LANE), lambda i: (i % nseq, 0))
    nk_out, nv_out = heads * MLA_QK, heads * MLA_V_DIM
    vmem = (2 * _nbytes((bm, kdim), ckv_n.dtype) + 2 * _nbytes(w_k.shape, w_k.dtype)
            + 2 * _nbytes(w_v.shape, w_v.dtype) + 8 * _nbytes((bm, LANE), jnp.float32)
            + 2 * _nbytes((bm, nk_out), jnp.bfloat16) + 2 * _nbytes((bm, nv_out), jnp.bfloat16))
    return pl.pallas_call(
        functools.partial(_mla_kv_body, heads),
        grid=(m // bm,),
        in_specs=[pl.BlockSpec((bm, kdim), lambda i: (i, 0)),
                  pl.BlockSpec(w_k.shape, lambda i: (0, 0)),
                  pl.BlockSpec(w_v.shape, lambda i: (0, 0)),
                  pl.BlockSpec((bm, LANE), lambda i: (i, kr_col_block)),
                  tab_spec, tab_spec, tab_spec],
        out_specs=[pl.BlockSpec((bm, nk_out), lambda i: (i, 0)),
                   pl.BlockSpec((bm, nv_out), lambda i: (i, 0))],
        out_shape=[jax.ShapeDtypeStruct((m, nk_out), jnp.bfloat16),
                   jax.ShapeDtypeStruct((m, nv_out), jnp.bfloat16)],
        compiler_params=_params(("parallel",), vmem),
        name="mla_kv",
    )(ckv_n, w_k, w_v, lat, *tables)


def _split_bf16(x):
    hi = x.astype(jnp.bfloat16)
    r = x - hi.astype(jnp.float32)
    mid = r.astype(jnp.bfloat16)
    lo = (r - mid.astype(jnp.float32)).astype(jnp.bfloat16)
    return hi, mid, lo


def _cum_body(logit_ref, b_ref, o_ref, carry_ref):
    @pl.when(pl.program_id(1) == 0)
    def _():
        carry_ref[...] = jnp.zeros_like(carry_ref)

    z = logit_ref[...] + b_ref[...]
    log_f = jnp.minimum(z, 0.0) - jnp.log1p(jnp.exp(-jnp.abs(z)))
    n = log_f.shape[0]
    tri = (lax.broadcasted_iota(jnp.int32, (n, n), 0)
           >= lax.broadcasted_iota(jnp.int32, (n, n), 1)).astype(jnp.bfloat16)
    cum = carry_ref[...]
    for part in _split_bf16(log_f):
        cum = cum + jnp.dot(tri, part, preferred_element_type=jnp.float32)
    o_ref[...] = cum
    carry_ref[...] = cum[n - 1:n, :]


def _forget_cumsum(logit, b_pad, batch, seq):
    bm = _tile(seq, CUM_BLOCK)
    ns = seq // bm
    vmem = 4 * _nbytes((bm, LANE), jnp.float32) + _nbytes((bm, bm), jnp.float32)
    return pl.pallas_call(
        _cum_body,
        grid=(batch, ns),
        in_specs=[pl.BlockSpec((bm, LANE), lambda b, s: (b * ns + s, 0)),
                  pl.BlockSpec((1, LANE), lambda b, s: (0, 0))],
        out_specs=pl.BlockSpec((bm, LANE), lambda b, s: (b * ns + s, 0)),
        out_shape=jax.ShapeDtypeStruct(logit.shape, jnp.float32),
        scratch_shapes=[pltpu.VMEM((1, LANE), jnp.float32)],
        compiler_params=_params(("parallel", "arbitrary"), vmem),
        name="forget_cumsum",
    )(logit, b_pad)


def _flash_body(scale, bq, bk, with_cum, *refs):
    if with_cum:
        q_ref, k_ref, v_ref, cc_ref, cr_ref, o_ref = refs
    else:
        q_ref, k_ref, v_ref, o_ref = refs
    i = pl.program_id(2)
    q = q_ref[...]
    dv = v_ref.shape[1]
    if with_cum:
        head = pl.program_id(1)
        cc = cc_ref[...]
        lane = lax.broadcasted_iota(jnp.int32, cc.shape, 1)
        cum_q = jnp.sum(jnp.where(lane == head, cc, 0.0), axis=1, keepdims=True)

    def step(j, carry, diagonal):
        m, l, acc = carry
        start = pl.multiple_of(j * bk, bk)
        k = k_ref[pl.ds(start, bk), :]
        v = v_ref[pl.ds(start, bk), :]
        s = lax.dot_general(q, k, (((1,), (1,)), ((), ())), preferred_element_type=jnp.float32) * scale
        if with_cum:
            s = s - cr_ref[j]
        if diagonal:
            row = lax.broadcasted_iota(jnp.int32, (bq, bk), 0)
            col = lax.broadcasted_iota(jnp.int32, (bq, bk), 1)
            s = jnp.where(row >= col, s, -jnp.inf)
        row_max = jnp.max(s, axis=1, keepdims=True)
        if with_cum:
            m_new = jnp.maximum(m, row_max + cum_q)
            shift = m_new - cum_q
        else:
            m_new = jnp.maximum(m, row_max)
            shift = m_new
        alpha = jnp.exp(m - m_new)
        p = jnp.exp(s - shift)
        l = alpha * l + jnp.sum(p, axis=1, keepdims=True)
        acc = alpha * acc + jnp.dot(p.astype(v.dtype), v, preferred_element_type=jnp.float32)
        return m_new, l, acc

    init = (jnp.full((bq, 1), -jnp.inf, jnp.float32), jnp.zeros((bq, 1), jnp.float32),
            jnp.zeros((bq, dv), jnp.float32))
    carry = lax.fori_loop(0, i, lambda j, c: step(j, c, False), init)
    m, l, acc = step(i, carry, True)
    o_ref[...] = (acc / l).astype(o_ref.dtype)


def _flash(name, q_arr, k_arr, v_arr, *, q_col0, k_col0, v_col0, dk, dv, heads, batch, seq, scale,
           cum_col=None, cum_row=None):
    bq = _tile(seq, FLASH_BQ)
    bk = bq
    nq = seq // bq
    with_cum = cum_col is not None
    in_specs = [pl.BlockSpec((bq, dk), lambda b, h, i: (b * nq + i, q_col0 + h)),
                pl.BlockSpec((seq, dk), lambda b, h, i: (b, k_col0 + h)),
                pl.BlockSpec((seq, dv), lambda b, h, i: (b, v_col0 + h))]
    args = [q_arr, k_arr, v_arr]
    vmem = (2 * _nbytes((bq, dk), q_arr.dtype) + 2 * _nbytes((seq, dk), k_arr.dtype)
            + 2 * _nbytes((seq, dv), v_arr.dtype) + 2 * _nbytes((bq, dv), jnp.bfloat16))
    if with_cum:
        in_specs += [pl.BlockSpec((bq, LANE), lambda b, h, i: (b * nq + i, 0)),
                     pl.BlockSpec((None, nq, 1, bk), lambda b, h, i: (b * heads + h, 0, 0, 0))]
        args += [cum_col, cum_row]
        vmem += 2 * _nbytes((bq, LANE), jnp.float32) + 2 * _nbytes((nq, 8, bk), jnp.float32)
    return pl.pallas_call(
        functools.partial(_flash_body, scale, bq, bk, with_cum),
        grid=(batch, heads, nq),
        in_specs=in_specs,
        out_specs=pl.BlockSpec((bq, dv), lambda b, h, i: (b * nq + i, h)),
        out_shape=jax.ShapeDtypeStruct((batch * seq, heads * dv), jnp.bfloat16),
        compiler_params=_params(("parallel", "parallel", "arbitrary"), vmem),
        name=name,
    )(*args)


def _swa_body(sink_ref, q_ref, kp_ref, kc_ref, vp_ref, vc_ref, o_ref):
    n = pl.program_id(1)
    blk = q_ref.shape[0]
    group = SWA_Q_HEADS // SWA_KV_HEADS
    d = SWA_HEAD_DIM
    scale = d ** -0.5
    row = lax.broadcasted_iota(jnp.int32, (blk, 2 * blk), 0)
    col = lax.broadcasted_iota(jnp.int32, (blk, 2 * blk), 1)
    dist = row + blk - col
    valid = (dist >= 0) & (dist < WINDOW) & ((col >= blk) | (n > 0))
    dist_f = dist.astype(jnp.float32)
    for g in range(SWA_KV_HEADS):
        k = jnp.concatenate([kp_ref[:, g * d:(g + 1) * d], kc_ref[:, g * d:(g + 1) * d]], axis=0)
        v = jnp.concatenate([vp_ref[:, g * d:(g + 1) * d], vc_ref[:, g * d:(g + 1) * d]], axis=0)
        for u in range(group):
            h = g * group + u
            slope = 2.0 ** (-8.0 * (h + 1) / SWA_Q_HEADS)
            sink = sink_ref[h]
            qh = q_ref[:, h * d:(h + 1) * d]
            s = lax.dot_general(qh, k, (((1,), (1,)), ((), ())), preferred_element_type=jnp.float32) * scale
            s = jnp.where(valid, s - slope * dist_f, -jnp.inf)
            m = jnp.maximum(jnp.max(s, axis=1, keepdims=True), sink)
            p = jnp.exp(s - m)
            denom = jnp.sum(p, axis=1, keepdims=True) + jnp.exp(sink - m)
            o = jnp.dot(p.astype(v.dtype), v, preferred_element_type=jnp.float32) / denom
            o_ref[:, h * d:(h + 1) * d] = o.astype(o_ref.dtype)


def _swa(sqkv, sinks, batch, seq):
    blk = WINDOW
    nb = seq // blk
    qw = SWA_Q_HEADS * SWA_HEAD_DIM
    kvw = SWA_KV_HEADS * SWA_HEAD_DIM
    k_col, v_col = qw // kvw, qw // kvw + 1
    cur = lambda col: pl.BlockSpec((blk, kvw), lambda b, n: (b * nb + n, col))
    prev = lambda col: pl.BlockSpec((blk, kvw), lambda b, n: (b * nb + jnp.maximum(n - 1, 0), col))
    vmem = 4 * _nbytes((blk, qw), jnp.bfloat16) + 8 * _nbytes((blk, kvw), jnp.bfloat16)
    return pl.pallas_call(
        _swa_body,
        grid=(batch, nb),
        in_specs=[pl.BlockSpec(memory_space=pltpu.SMEM),
                  pl.BlockSpec((blk, qw), lambda b, n: (b * nb + n, 0)),
                  prev(k_col), cur(k_col), prev(v_col), cur(v_col)],
        out_specs=pl.BlockSpec((blk, qw), lambda b, n: (b * nb + n, 0)),
        out_shape=jax.ShapeDtypeStruct((batch * seq, qw), jnp.bfloat16),
        compiler_params=_params(("parallel", "parallel"), vmem),
        name="swa",
    )(sinks.astype(jnp.float32), sqkv, sqkv, sqkv, sqkv, sqkv)


def _router_body(x_ref, g_ref, w_ref, comb_ref):
    x = x_ref[...]
    xn = x * lax.rsqrt(jnp.mean(x * x, axis=-1, keepdims=True) + RMS_EPS) * g_ref[...]
    logits = jnp.dot(xn, w_ref[...], preferred_element_type=jnp.float32,
                     precision=lax.Precision.HIGHEST)
    lane = lax.broadcasted_iota(jnp.int32, logits.shape, 1)
    logits = jnp.where(lane < N_EXPERTS, logits, -jnp.inf)
    top1 = jnp.max(logits, axis=1, keepdims=True)
    idx1 = jnp.min(jnp.where(logits == top1, lane, LANE), axis=1, keepdims=True)
    rest = jnp.where(lane == idx1, -jnp.inf, logits)
    top2 = jnp.max(rest, axis=1, keepdims=True)
    idx2 = jnp.min(jnp.where(rest == top2, lane, LANE), axis=1, keepdims=True)
    e2 = jnp.exp(top2 - top1)
    w1 = 1.0 / (1.0 + e2)
    w2 = e2 / (1.0 + e2)
    comb_ref[...] = jnp.where(lane == idx1, w1, 0.0) + jnp.where(lane == idx2, w2, 0.0)


def _router(x, g, w_router):
    rows, d = x.shape
    bm = _tile(rows, ROUTER_BM)
    w_pad = jnp.pad(w_router.astype(jnp.float32), ((0, 0), (0, LANE - w_router.shape[1])))
    vmem = 2 * _nbytes((bm, d), jnp.float32) + 2 * _nbytes((d, LANE), jnp.float32) + 2 * _nbytes((bm, LANE), jnp.float32)
    return pl.pallas_call(
        _router_body,
        grid=(rows // bm,),
        in_specs=[pl.BlockSpec((bm, d), lambda i: (i, 0)),
                  pl.BlockSpec((1, d), lambda i: (0, 0)),
                  pl.BlockSpec((d, LANE), lambda i: (0, 0))],
        out_specs=pl.BlockSpec((bm, LANE), lambda i: (i, 0)),
        out_shape=jax.ShapeDtypeStruct((rows, LANE), jnp.float32),
        compiler_params=_params(("parallel",), vmem),
        name="router",
    )(x, g.reshape(1, d).astype(jnp.float32), w_pad)


def _swiglu_up(name, xn, w_gate, w_up, comb=None):
    m, d = xn.shape
    n_exp, _, f = w_gate.shape
    bm, bn = _tile(m, MM_BM), _tile(f, MM_BN_MULTI)
    nj = f // bn
    w_spec = pl.BlockSpec((None, d, bn), lambda i, j, k: (j // nj, 0, j % nj))
    x_ops, epilogue = [], _ep_swiglu
    if comb is not None:
        x_ops = [(comb, pl.BlockSpec((bm, LANE), lambda i, j, k: (i, 0)))]

        def epilogue(vals, comb_tile):
            expert = pl.program_id(1) // nj
            lane = lax.broadcasted_iota(jnp.int32, comb_tile.shape, 1)
            scale = jnp.sum(jnp.where(lane == expert, comb_tile, 0.0), axis=1, keepdims=True)
            return _ep_swiglu(vals) * scale

    return _mm(name, [(xn, _a_spec(bm, d))], [(w_gate, w_spec), (w_up, w_spec)], x_ops,
               [(0, 0), (0, 1)], epilogue, m=m, n=n_exp * f, bm=bm, bn=bn, nk=1,
               out_dtype=jnp.bfloat16)


def _bf16(w):
    return w.astype(jnp.bfloat16)


def _mixer(x2, batch, seq, g_mix, w_in, b_forget, b_gate, g_q, g_kv, w_uq, w_ukv, sinks, w_branch,
           w_out, tables):
    d = x2.shape[1]
    fox_w = FOX_HEADS * FOX_HEAD_DIM
    swa_w = SWA_Q_HEADS * SWA_HEAD_DIM
    swa_kv_w = SWA_KV_HEADS * SWA_HEAD_DIM
    mla_w = MLA_HEADS * MLA_V_DIM
    o_lat = 0
    o_fox = Q_LORA + KV_LORA + ROPE_DIM
    o_flog = o_fox + 3 * fox_w
    o_swa = o_flog + FOX_HEADS
    o_gate = o_swa + swa_w + 2 * swa_kv_w

    lat_w = Q_LORA + KV_LORA + LANE
    lat_pad = -(lat_w) % 512
    w_lat = _bf16(jnp.pad(w_in[:, o_lat:o_fox], ((0, 0), (0, LANE - ROPE_DIM + lat_pad))))
    w_fox = _bf16(w_in[:, o_fox:o_flog])
    w_flog = _bf16(jnp.pad(w_in[:, o_flog:o_swa], ((0, 0), (0, LANE - FOX_HEADS))))
    w_swa = _bf16(w_in[:, o_swa:o_gate])
    w_gate = _bf16(w_in[:, o_gate:])

    xn = _rmsnorm(x2, g_mix, jnp.bfloat16, name="norm_mix")
    lat = _linear("in_latent", xn, w_lat, jnp.float32, bn=512)
    fqkv = _linear("in_fox", xn, w_fox, jnp.bfloat16)
    flog = _linear("in_forget", xn, w_flog, jnp.float32)
    sqkv = _linear("in_swa", xn, w_swa, jnp.bfloat16, bn=512)
    gates = _linear("in_gate", xn, w_gate, jnp.bfloat16, epilogue=_ep_sigmoid_bias,
                    extras=[(b_gate.reshape(1, -1).astype(jnp.float32), "row")])

    cq_n = _rmsnorm(lat, g_q, jnp.bfloat16, col_block=0, width=Q_LORA, name="norm_q")
    ckv_n = _rmsnorm(lat, g_kv, jnp.bfloat16, col_block=Q_LORA // KV_LORA, width=KV_LORA, name="norm_kv")
    w_uq_p = w_uq.reshape(Q_LORA, MLA_HEADS, NOPE_DIM + ROPE_DIM)
    w_uq_p = _bf16(jnp.pad(w_uq_p, ((0, 0), (0, 0), (0, ROPE_DIM))).reshape(Q_LORA, MLA_HEADS * MLA_QK))
    w_ukv_h = w_ukv.reshape(KV_LORA, MLA_HEADS, NOPE_DIM + MLA_V_DIM)
    w_k = _bf16(w_ukv_h[:, :, :NOPE_DIM].reshape(KV_LORA, MLA_HEADS * NOPE_DIM))
    w_v = _bf16(w_ukv_h[:, :, NOPE_DIM:].reshape(KV_LORA, MLA_HEADS * MLA_V_DIM))
    q_mla = _mla_q(cq_n, w_uq_p, tables, seq)
    k_mla, v_mla = _mla_kv(ckv_n, w_k, w_v, lat, (Q_LORA + KV_LORA) // LANE, tables, seq)
    o_mla = _flash("attn_mla", q_mla, k_mla, v_mla, q_col0=0, k_col0=0, v_col0=0, dk=MLA_QK,
                   dv=MLA_V_DIM, heads=MLA_HEADS, batch=batch, seq=seq,
                   scale=(NOPE_DIM + ROPE_DIM) ** -0.5)

    b_pad = jnp.pad(b_forget.astype(jnp.float32), (0, LANE - FOX_HEADS)).reshape(1, LANE)
    cum = _forget_cumsum(flog, b_pad, batch, seq)
    bk = _tile(seq, FLASH_BK)
    cum_row = cum[:, :FOX_HEADS].reshape(batch, seq, FOX_HEADS).transpose(0, 2, 1)
    cum_row = cum_row.reshape(batch * FOX_HEADS, seq // bk, 1, bk)
    o_foxa = _flash("attn_fox", fqkv, fqkv, fqkv, q_col0=0, k_col0=FOX_HEADS, v_col0=2 * FOX_HEADS,
                    dk=FOX_HEAD_DIM, dv=FOX_HEAD_DIM, heads=FOX_HEADS, batch=batch, seq=seq,
                    scale=FOX_HEAD_DIM ** -0.5, cum_col=cum, cum_row=cum_row)

    o_swa_a = _swa(sqkv, sinks, batch, seq)

    m = x2.shape[0]
    bm, bn = _tile(m, MM_BM), _tile(d, MM_BN_MULTI)
    nj = d // bn
    wb = [_bf16(w_branch[:mla_w]), _bf16(w_branch[mla_w:mla_w + fox_w]), _bf16(w_branch[mla_w + fox_w:])]
    outs = [o_mla, o_foxa, o_swa_a]
    merged = _mm("branch_merge",
                 [(o, _a_spec(bm, o.shape[1])) for o in outs],
                 [(w, _w_spec(w.shape[0], bn)) for w in wb],
                 [(gates, _tile_spec(bm, bn, col_off=b * nj)) for b in range(N_BRANCHES)],
                 [(0, 0), (1, 1), (2, 2)], _ep_merge,
                 m=m, n=d, bm=bm, bn=bn, nk=1, out_dtype=jnp.bfloat16)
    return _linear("out_proj", merged, _bf16(w_out), jnp.float32, epilogue=_ep_residual,
                   extras=[(x2, "tile")])


def kernel(x, g_mix_norm, w_in, b_forget, b_gate, g_q_norm, g_kv_norm, w_uq, w_ukv, sinks, w_branch,
           w_out, g_ffn_norm, w_dense_gate, w_dense_up, w_dense_down, w_router, w_exp_gate, w_exp_up,
           w_exp_down, g_final):
    batch, seq, d = x.shape
    depth = w_in.shape[0]
    x2 = x.reshape(batch * seq, d)
    tables = _rope_tables(seq)
    for l in range(depth):
        x2 = _mixer(x2, batch, seq, g_mix_norm[l], w_in[l], b_forget[l], b_gate[l], g_q_norm[l],
                    g_kv_norm[l], w_uq[l], w_ukv[l], sinks[l], w_branch[l], w_out[l], tables)
        j = l // 2
        xn = _rmsnorm(x2, g_ffn_norm[l], jnp.bfloat16, name="norm_ffn")
        if l % 2 == 0:
            h = _swiglu_up("dense_up", xn, _bf16(w_dense_gate[j])[None], _bf16(w_dense_up[j])[None])
            w_down = _bf16(w_dense_down[j])
        else:
            comb = _router(x2, g_ffn_norm[l], w_router[j])
            h = _swiglu_up("moe_up", xn, _bf16(w_exp_gate[j]), _bf16(w_exp_up[j]), comb)
            w_down = _bf16(w_exp_down[j]).reshape(-1, d)
        x2 = _linear("ffn_down", h, w_down, jnp.float32, epilogue=_ep_residual, extras=[(x2, "tile")])
    out = _rmsnorm(x2, g_final, x.dtype, name="norm_final")
    return out.reshape(batch, seq, d)
```

```python
import functools

import jax
import jax.numpy as jnp
from jax import lax
from jax.experimental import pallas as pl
from jax.experimental.pallas import tpu as pltpu

RMS_EPS = 1e-6
MLA_HEADS = 16
Q_LORA = 1536
KV_LORA = 512
NOPE_DIM = 128
ROPE_DIM = 64
MLA_V_DIM = 128
ROPE_THETA = 10000.0
FOX_HEADS = 16
FOX_HEAD_DIM = 128
SWA_Q_HEADS = 32
SWA_KV_HEADS = 4
SWA_HEAD_DIM = 64
WINDOW = 128
N_BRANCHES = 3
N_EXPERTS = 8
TOP_K = 2

LANE = 128
V7X_VMEM_BYTES = 64 * 1024 * 1024
V7X_VMEM_BUDGET = 56 * 1024 * 1024
COMPILER_SCRATCH_BYTES = 12 * 1024 * 1024

MM_BM = 1024
MM_BN = 1024
MM_BK = 2048
MM_FULL_K = 4096
MM_BN_MULTI = 512
NORM_BM = 512
FLASH_BQ = 512
FLASH_BK = 512
FLASH_HEADS_PER_STEP = 2
CUM_BLOCK = 512
MLA_KV_BM = 512
ROUTER_BM = 512
MOE_ROW_TILE = 512
MOE_COMBINE_TILE = 256

MLA_QK = NOPE_DIM + 2 * ROPE_DIM
LOG2E = 1.4426950408889634


def _params(semantics, vmem_bytes):
    limit = min(V7X_VMEM_BUDGET, vmem_bytes + COMPILER_SCRATCH_BYTES)
    return pltpu.CompilerParams(dimension_semantics=semantics, vmem_limit_bytes=int(limit))


def _nbytes(shape, dtype):
    n = 1
    for s in shape:
        n *= s
    return n * jnp.dtype(dtype).itemsize


def _tile(full, want):
    t = min(full, want)
    while full % t:
        t //= 2
    return t


def _rmsnorm_body(x_ref, g_ref, o_ref):
    x = x_ref[...].astype(jnp.float32)
    y = x * lax.rsqrt(jnp.mean(x * x, axis=-1, keepdims=True) + RMS_EPS)
    o_ref[...] = (y * g_ref[...]).astype(o_ref.dtype)


def _rmsnorm(x, g, out_dtype, *, col_block=0, width=None, name="rmsnorm"):
    rows = x.shape[0]
    width = x.shape[1] if width is None else width
    bm = _tile(rows, NORM_BM)
    vmem = 2 * _nbytes((bm, width), x.dtype) + 2 * _nbytes((bm, width), out_dtype)
    return pl.pallas_call(
        _rmsnorm_body,
        grid=(rows // bm,),
        in_specs=[pl.BlockSpec((bm, width), lambda i: (i, col_block)),
                  pl.BlockSpec((1, width), lambda i: (0, 0))],
        out_specs=pl.BlockSpec((bm, width), lambda i: (i, 0)),
        out_shape=jax.ShapeDtypeStruct((rows, width), out_dtype),
        compiler_params=_params(("parallel",), vmem),
        name=name,
    )(x, g.reshape(1, width).astype(jnp.float32))


def _mm_body(n_a, n_w, n_x, dots, nk, epilogue, *refs):
    a = refs[:n_a]
    w = refs[n_a:n_a + n_w]
    xs = refs[n_a + n_w:n_a + n_w + n_x]
    o = refs[n_a + n_w + n_x]
    accs = refs[n_a + n_w + n_x + 1:]

    def products():
        return [jnp.dot(a[ai][...], w[wi][...], preferred_element_type=jnp.float32)
                for ai, wi in dots]

    def finish(vals):
        o[...] = epilogue(vals, *[x[...] for x in xs]).astype(o.dtype)

    if nk == 1:
        finish(products())
        return

    k = pl.program_id(2)

    @pl.when(k == 0)
    def _():
        for acc, p in zip(accs, products()):
            acc[...] = p

    @pl.when((k > 0) & (k < nk - 1))
    def _():
        for acc, p in zip(accs, products()):
            acc[...] += p

    @pl.when(k == nk - 1)
    def _():
        finish([acc[...] + p for acc, p in zip(accs, products())])


def _mm(name, a_ops, w_ops, x_ops, dots, epilogue, *, m, n, bm, bn, nk, out_dtype):
    ops = list(a_ops) + list(w_ops) + list(x_ops)
    vmem = 2 * _nbytes((bm, bn), out_dtype)
    for arr, spec in ops:
        blk = [d for d in spec.block_shape if d is not None]
        vmem += 2 * _nbytes(blk, arr.dtype)
    scratch = []
    if nk > 1:
        scratch = [pltpu.VMEM((bm, bn), jnp.float32) for _ in dots]
        vmem += len(dots) * _nbytes((bm, bn), jnp.float32)
    body = functools.partial(_mm_body, len(a_ops), len(w_ops), len(x_ops), tuple(dots), nk, epilogue)
    return pl.pallas_call(
        body,
        grid=(m // bm, n // bn, nk),
        in_specs=[spec for _, spec in ops],
        out_specs=pl.BlockSpec((bm, bn), lambda i, j, k: (i, j)),
        out_shape=jax.ShapeDtypeStruct((m, n), out_dtype),
        scratch_shapes=scratch,
        compiler_params=_params(("parallel", "parallel", "arbitrary"), vmem),
        name=name,
    )(*[arr for arr, _ in ops])


def _a_spec(bm, bk):
    return pl.BlockSpec((bm, bk), lambda i, j, k: (i, k))


def _w_spec(bk, bn):
    return pl.BlockSpec((bk, bn), lambda i, j, k: (k, j))


def _tile_spec(bm, bn, col_off=0):
    return pl.BlockSpec((bm, bn), lambda i, j, k: (i, j + col_off))


def _row_spec(bn, col_off=0):
    return pl.BlockSpec((1, bn), lambda i, j, k: (0, j + col_off))


def _ep_plain(vals):
    return vals[0]


def _ep_residual(vals, res):
    return res + vals[0]


def _ep_sigmoid_bias(vals, bias):
    return jax.nn.sigmoid(vals[0] + bias)


def _ep_scale_leading(n_blocks, factor, vals):
    return vals[0] * jnp.where(pl.program_id(1) < n_blocks, factor, 1.0)


def _ep_swiglu(vals):
    return jax.nn.silu(vals[0]) * vals[1]


def _ep_merge(vals, g0, g1, g2):
    return (g0.astype(jnp.float32) * vals[0] + g1.astype(jnp.float32) * vals[1]
            + g2.astype(jnp.float32) * vals[2])


def _linear(name, a, w, out_dtype, *, epilogue=_ep_plain, extras=(), bn=MM_BN):
    m, kdim = a.shape
    n = w.shape[1]
    bm, bn, bk = _tile(m, MM_BM), _tile(n, bn), _tile(kdim, MM_BK if kdim > MM_FULL_K else kdim)
    x_ops = []
    for arr, kind in extras:
        x_ops.append((arr, _tile_spec(bm, bn) if kind == "tile" else _row_spec(bn)))
    return _mm(name, [(a, _a_spec(bm, bk))], [(w, _w_spec(bk, bn))], x_ops, [(0, 0)], epilogue,
               m=m, n=n, bm=bm, bn=bn, nk=kdim // bk, out_dtype=out_dtype)


def _rope_tile(t, cos_t, sin_a, sin_b):
    half = ROPE_DIM // 2
    return (t * cos_t + pltpu.roll(t, LANE - half, axis=1) * sin_a
            + pltpu.roll(t, half, axis=1) * sin_b)


def _rope_tables(seq):
    half = ROPE_DIM // 2
    inv = ROPE_THETA ** (-jnp.arange(half, dtype=jnp.float32) / half)
    ang = jnp.arange(seq, dtype=jnp.float32)[:, None] * inv[None, :]
    cos, sin = jnp.cos(ang), jnp.sin(ang)
    z = jnp.zeros_like(cos)
    pad = jnp.zeros((seq, LANE - 2 * half), jnp.float32)
    cos_t = jnp.concatenate([cos, cos, pad], axis=1)
    sin_a = jnp.concatenate([-sin, z, pad], axis=1)
    sin_b = jnp.concatenate([z, sin, pad], axis=1)
    return cos_t, sin_a, sin_b


def _ep_mla_q(vals, cos_t, sin_a, sin_b):
    acc = vals[0] * ((NOPE_DIM + ROPE_DIM) ** -0.5 * LOG2E)
    pieces = []
    for h in range(acc.shape[1] // MLA_QK):
        lo = h * MLA_QK
        pieces.append(acc[:, lo:lo + NOPE_DIM])
        pieces.append(_rope_tile(acc[:, lo + NOPE_DIM:lo + MLA_QK], cos_t, sin_a, sin_b))
    return jnp.concatenate(pieces, axis=1)


def _mla_q(cq_n, w_uq_p, tables, seq):
    m, kdim = cq_n.shape
    n = w_uq_p.shape[1]
    bm = _tile(seq, MM_BM)
    bn = _tile(n, MM_BN)
    nseq = seq // bm
    tab_spec = pl.BlockSpec((bm, LANE), lambda i, j, k: (i % nseq, 0))
    return _mm("mla_q", [(cq_n, _a_spec(bm, kdim))], [(w_uq_p, _w_spec(kdim, bn))],
               [(t, tab_spec) for t in tables], [(0, 0)], _ep_mla_q,
               m=m, n=n, bm=bm, bn=bn, nk=1, out_dtype=jnp.bfloat16)


def _mla_kv_body(heads, a_ref, wk_ref, wv_ref, kr_ref, cos_ref, sina_ref, sinb_ref, k_ref, v_ref):
    a = a_ref[...]
    kn = jnp.dot(a, wk_ref[...], preferred_element_type=jnp.float32)
    v_ref[...] = jnp.dot(a, wv_ref[...], preferred_element_type=jnp.float32).astype(v_ref.dtype)
    kr = _rope_tile(kr_ref[...], cos_ref[...], sina_ref[...], sinb_ref[...]).astype(k_ref.dtype)
    for h in range(heads):
        k_ref[:, h * MLA_QK:h * MLA_QK + NOPE_DIM] = kn[:, h * NOPE_DIM:(h + 1) * NOPE_DIM].astype(k_ref.dtype)
        k_ref[:, h * MLA_QK + NOPE_DIM:(h + 1) * MLA_QK] = kr


def _mla_kv(ckv_n, w_k, w_v, lat, kr_col_block, tables, seq):
    m, kdim = ckv_n.shape
    heads = w_k.shape[1] // NOPE_DIM
    bm = _tile(seq, MLA_KV_BM)
    nseq = seq // bm
    tab_spec = pl.BlockSpec((bm, LANE), lambda i: (i % nseq, 0))
    nk_out, nv_out = heads * MLA_QK, heads * MLA_V_DIM
    vmem = (2 * _nbytes((bm, kdim), ckv_n.dtype) + 2 * _nbytes(w_k.shape, w_k.dtype)
            + 2 * _nbytes(w_v.shape, w_v.dtype) + 8 * _nbytes((bm, LANE), jnp.float32)
            + 2 * _nbytes((bm, nk_out), jnp.bfloat16) + 2 * _nbytes((bm, nv_out), jnp.bfloat16))
    return pl.pallas_call(
        functools.partial(_mla_kv_body, heads),
        grid=(m // bm,),
        in_specs=[pl.BlockSpec((bm, kdim), lambda i: (i, 0)),
                  pl.BlockSpec(w_k.shape, lambda i: (0, 0)),
                  pl.BlockSpec(w_v.shape, lambda i: (0, 0)),
                  pl.BlockSpec((bm, LANE), lambda i: (i, kr_col_block)),
                  tab_spec, tab_spec, tab_spec],
        out_specs=[pl.BlockSpec((bm, nk_out), lambda i: (i, 0)),
                   pl.BlockSpec((bm, nv_out), lambda i: (i, 0))],
        out_shape=[jax.ShapeDtypeStruct((m, nk_out), jnp.bfloat16),
                   jax.ShapeDtypeStruct((m, nv_out), jnp.bfloat16)],
        compiler_params=_params(("parallel",), vmem),
        name="mla_kv",
    )(ckv_n, w_k, w_v, lat, *tables)


def _split_bf16(x):
    hi = x.astype(jnp.bfloat16)
    r = x - hi.astype(jnp.float32)
    mid = r.astype(jnp.bfloat16)
    lo = (r - mid.astype(jnp.float32)).astype(jnp.bfloat16)
    return hi, mid, lo


def _cum_body(logit_ref, b_ref, o_ref, carry_ref):
    @pl.when(pl.program_id(1) == 0)
    def _():
        carry_ref[...] = jnp.zeros_like(carry_ref)

    z = logit_ref[...] + b_ref[...]
    log_f = (jnp.minimum(z, 0.0) - jnp.log1p(jnp.exp(-jnp.abs(z)))) * LOG2E
    n = log_f.shape[0]
    tri = (lax.broadcasted_iota(jnp.int32, (n, n), 0)
           >= lax.broadcasted_iota(jnp.int32, (n, n), 1)).astype(jnp.bfloat16)
    cum = carry_ref[...]
    for part in _split_bf16(log_f):
        cum = cum + jnp.dot(tri, part, preferred_element_type=jnp.float32)
    o_ref[...] = cum
    carry_ref[...] = cum[n - 1:n, :]


def _forget_cumsum(logit, b_pad, batch, seq):
    bm = _tile(seq, CUM_BLOCK)
    ns = seq // bm
    vmem = 4 * _nbytes((bm, LANE), jnp.float32) + _nbytes((bm, bm), jnp.float32)
    return pl.pallas_call(
        _cum_body,
        grid=(batch, ns),
        in_specs=[pl.BlockSpec((bm, LANE), lambda b, s: (b * ns + s, 0)),
                  pl.BlockSpec((1, LANE), lambda b, s: (0, 0))],
        out_specs=pl.BlockSpec((bm, LANE), lambda b, s: (b * ns + s, 0)),
        out_shape=jax.ShapeDtypeStruct(logit.shape, jnp.float32),
        scratch_shapes=[pltpu.VMEM((1, LANE), jnp.float32)],
        compiler_params=_params(("parallel", "arbitrary"), vmem),
        name="forget_cumsum",
    )(logit, b_pad)


def _flash_body(bq, bk, hp, dk, dv, with_cum, *refs):
    if with_cum:
        q_ref, k_ref, v_ref, cc_ref, cr_ref, o_ref = refs
    else:
        q_ref, k_ref, v_ref, o_ref = refs
    i = pl.program_id(2)
    qs = [q_ref[:, h * dk:(h + 1) * dk] for h in range(hp)]
    if with_cum:
        head0 = pl.program_id(1) * hp
        cc = cc_ref[...]
        lane = lax.broadcasted_iota(jnp.int32, cc.shape, 1)
        cum_q = [jnp.sum(jnp.where(lane == head0 + h, cc, 0.0), axis=1, keepdims=True)
                 for h in range(hp)]

    def step(j, carry, diagonal):
        start = pl.multiple_of(j * bk, bk)
        if diagonal:
            causal = (lax.broadcasted_iota(jnp.int32, (bq, bk), 0)
                      >= lax.broadcasted_iota(jnp.int32, (bq, bk), 1))
        out = []
        for h in range(hp):
            m, l, acc = carry[h]
            k = k_ref[pl.ds(start, bk), h * dk:(h + 1) * dk]
            v = v_ref[pl.ds(start, bk), h * dv:(h + 1) * dv]
            s = lax.dot_general(qs[h], k, (((1,), (1,)), ((), ())), preferred_element_type=jnp.float32)
            if with_cum:
                s = s - cr_ref[h, j]
            if diagonal:
                s = jnp.where(causal, s, -jnp.inf)
            row_max = jnp.max(s, axis=1, keepdims=True)
            if with_cum:
                m_new = jnp.maximum(m, row_max + cum_q[h])
                shift = m_new - cum_q[h]
            else:
                m_new = jnp.maximum(m, row_max)
                shift = m_new
            alpha = jnp.exp2(m - m_new)
            p = jnp.exp2(s - shift)
            l = alpha * l + jnp.sum(p, axis=1, keepdims=True)
            acc = alpha * acc + jnp.dot(p.astype(v.dtype), v, preferred_element_type=jnp.float32)
            out.append((m_new, l, acc))
        return tuple(out)

    init = tuple((jnp.full((bq, 1), -jnp.inf, jnp.float32), jnp.zeros((bq, 1), jnp.float32),
                  jnp.zeros((bq, dv), jnp.float32)) for _ in range(hp))
    carry = lax.fori_loop(0, i, lambda j, c: step(j, c, False), init)
    final = step(i, carry, True)
    for h in range(hp):
        _, l, acc = final[h]
        o_ref[:, h * dv:(h + 1) * dv] = (acc / l).astype(o_ref.dtype)


def _flash(name, q_arr, k_arr, v_arr, *, q_col0, k_col0, v_col0, dk, dv, heads, batch, seq,
           cum_col=None, cum_row=None):
    bq = _tile(seq, FLASH_BQ)
    bk = bq
    nq = seq // bq
    hp = FLASH_HEADS_PER_STEP
    assert heads % hp == 0 and q_col0 % hp == 0 and k_col0 % hp == 0 and v_col0 % hp == 0
    with_cum = cum_col is not None
    in_specs = [pl.BlockSpec((bq, hp * dk), lambda b, g, i: (b * nq + i, q_col0 // hp + g)),
                pl.BlockSpec((seq, hp * dk), lambda b, g, i: (b, k_col0 // hp + g)),
                pl.BlockSpec((seq, hp * dv), lambda b, g, i: (b, v_col0 // hp + g))]
    args = [q_arr, k_arr, v_arr]
    vmem = (2 * _nbytes((bq, hp * dk), q_arr.dtype) + 2 * _nbytes((seq, hp * dk), k_arr.dtype)
            + 2 * _nbytes((seq, hp * dv), v_arr.dtype) + 2 * _nbytes((bq, hp * dv), jnp.bfloat16))
    if with_cum:
        in_specs += [pl.BlockSpec((bq, LANE), lambda b, g, i: (b * nq + i, 0)),
                     pl.BlockSpec((hp, nq, 1, bk), lambda b, g, i: (b * (heads // hp) + g, 0, 0, 0))]
        args += [cum_col, cum_row]
        vmem += 2 * _nbytes((bq, LANE), jnp.float32) + 2 * _nbytes((hp, nq, 8, bk), jnp.float32)
    return pl.pallas_call(
        functools.partial(_flash_body, bq, bk, hp, dk, dv, with_cum),
        grid=(batch, heads // hp, nq),
        in_specs=in_specs,
        out_specs=pl.BlockSpec((bq, hp * dv), lambda b, g, i: (b * nq + i, g)),
        out_shape=jax.ShapeDtypeStruct((batch * seq, heads * dv), jnp.bfloat16),
        compiler_params=_params(("parallel", "parallel", "arbitrary"), vmem),
        name=name,
    )(*args)


def _block_diag_pair(slab, head_in_slab):
    d = SWA_HEAD_DIM
    lane = lax.broadcasted_iota(jnp.int32, slab.shape, 1)
    x32 = slab.astype(jnp.float32)
    if head_in_slab == 0:
        left = jnp.where(lane < d, x32, 0.0)
        right = pltpu.roll(left, d, axis=1)
    else:
        right = jnp.where(lane >= d, x32, 0.0)
        left = pltpu.roll(right, d, axis=1)
    return jnp.concatenate([left, right], axis=0).astype(slab.dtype)


def _swa_body(sink_ref, q_ref, kp_ref, kc_ref, vp_ref, vc_ref, o_ref):
    n = pl.program_id(1)
    blk = q_ref.shape[0]
    group = SWA_Q_HEADS // SWA_KV_HEADS
    d = SWA_HEAD_DIM
    keys = 2 * blk
    row = lax.broadcasted_iota(jnp.int32, (blk, keys), 0)
    col = lax.broadcasted_iota(jnp.int32, (blk, keys), 1)
    dist = row + blk - col
    valid = (dist >= 0) & (dist < WINDOW) & ((col >= blk) | (n > 0))
    dist_f = dist.astype(jnp.float32)
    out_lane = lax.broadcasted_iota(jnp.int32, (blk, 2 * d), 1)
    for g in range(SWA_KV_HEADS):
        slab = slice((g // 2) * 2 * d, (g // 2 + 1) * 2 * d)
        k2 = _block_diag_pair(jnp.concatenate([kp_ref[:, slab], kc_ref[:, slab]], axis=0), g % 2)
        v2 = _block_diag_pair(jnp.concatenate([vp_ref[:, slab], vc_ref[:, slab]], axis=0), g % 2)
        for u in range(0, group, 2):
            h = g * group + u
            slopes = [LOG2E * 2.0 ** (-8.0 * (h + c + 1) / SWA_Q_HEADS) for c in range(2)]
            sinks = [sink_ref[h + c] * LOG2E for c in range(2)]
            q2 = q_ref[:, h * d:(h + 2) * d]
            s = lax.dot_general(q2, k2, (((1,), (1,)), ((), ())), preferred_element_type=jnp.float32)
            ps, denoms = [], []
            for c in range(2):
                s_c = jnp.where(valid, s[:, c * keys:(c + 1) * keys] - slopes[c] * dist_f, -jnp.inf)
                m = jnp.maximum(jnp.max(s_c, axis=1, keepdims=True), sinks[c])
                p = jnp.exp2(s_c - m)
                ps.append(p)
                denoms.append(jnp.sum(p, axis=1, keepdims=True) + jnp.exp2(sinks[c] - m))
            p2 = jnp.concatenate(ps, axis=1).astype(v2.dtype)
            o = jnp.dot(p2, v2, preferred_element_type=jnp.float32)
            o = o / jnp.where(out_lane < d, denoms[0], denoms[1])
            o_ref[:, h * d:(h + 2) * d] = o.astype(o_ref.dtype)


def _swa(sqkv, sinks, batch, seq):
    blk = WINDOW
    nb = seq // blk
    qw = SWA_Q_HEADS * SWA_HEAD_DIM
    kvw = SWA_KV_HEADS * SWA_HEAD_DIM
    k_col, v_col = qw // kvw, qw // kvw + 1
    cur = lambda col: pl.BlockSpec((blk, kvw), lambda b, n: (b * nb + n, col))
    prev = lambda col: pl.BlockSpec((blk, kvw), lambda b, n: (b * nb + jnp.maximum(n - 1, 0), col))
    vmem = 4 * _nbytes((blk, qw), jnp.bfloat16) + 8 * _nbytes((blk, kvw), jnp.bfloat16)
    return pl.pallas_call(
        _swa_body,
        grid=(batch, nb),
        in_specs=[pl.BlockSpec(memory_space=pltpu.SMEM),
                  pl.BlockSpec((blk, qw), lambda b, n: (b * nb + n, 0)),
                  prev(k_col), cur(k_col), prev(v_col), cur(v_col)],
        out_specs=pl.BlockSpec((blk, qw), lambda b, n: (b * nb + n, 0)),
        out_shape=jax.ShapeDtypeStruct((batch * seq, qw), jnp.bfloat16),
        compiler_params=_params(("parallel", "parallel"), vmem),
        name="swa",
    )(sinks.astype(jnp.float32), sqkv, sqkv, sqkv, sqkv, sqkv)


def _router_body(x_ref, g_ref, w_ref, eid_ref, rank_ref, wts_ref, cnt_ref, base_ref):
    @pl.when(pl.program_id(0) == 0)
    def _():
        base_ref[...] = jnp.zeros_like(base_ref)

    x = x_ref[...]
    xn = x * lax.rsqrt(jnp.mean(x * x, axis=-1, keepdims=True) + RMS_EPS) * g_ref[...]
    logits = jnp.dot(xn, w_ref[...], preferred_element_type=jnp.float32,
                     precision=lax.Precision.HIGHEST)
    lane = lax.broadcasted_iota(jnp.int32, logits.shape, 1)
    logits = jnp.where(lane < N_EXPERTS, logits, -jnp.inf)
    top1 = jnp.max(logits, axis=1, keepdims=True)
    idx1 = jnp.min(jnp.where(logits == top1, lane, LANE), axis=1, keepdims=True)
    rest = jnp.where(lane == idx1, -jnp.inf, logits)
    top2 = jnp.max(rest, axis=1, keepdims=True)
    idx2 = jnp.min(jnp.where(rest == top2, lane, LANE), axis=1, keepdims=True)
    e2 = jnp.exp(top2 - top1)
    w1 = 1.0 / (1.0 + e2)
    w2 = e2 / (1.0 + e2)
    n = x.shape[0]
    hot = (lane == idx1) | (lane == idx2)
    earlier = (lax.broadcasted_iota(jnp.int32, (n, n), 0)
               > lax.broadcasted_iota(jnp.int32, (n, n), 1)).astype(jnp.bfloat16)
    before = base_ref[...] + jnp.dot(earlier, hot.astype(jnp.bfloat16), preferred_element_type=jnp.float32)
    rank1 = jnp.sum(jnp.where(lane == idx1, before, 0.0), axis=1, keepdims=True)
    rank2 = jnp.sum(jnp.where(lane == idx2, before, 0.0), axis=1, keepdims=True)
    eid_ref[...] = jnp.where(lane == 0, idx1, jnp.where(lane == 1, idx2, 0))
    rank_ref[...] = jnp.where(lane == 0, rank1, jnp.where(lane == 1, rank2, 0.0)).astype(jnp.int32)
    wts_ref[...] = jnp.where(lane == 0, w1, jnp.where(lane == 1, w2, 0.0))
    total = base_ref[...] + jnp.sum(hot.astype(jnp.float32), axis=0, keepdims=True)
    base_ref[...] = total
    cnt_ref[...] = total.astype(jnp.int32)


def _router(x, g, w_router):
    rows, d = x.shape
    bm = _tile(rows, ROUTER_BM)
    w_pad = jnp.pad(w_router.astype(jnp.float32), ((0, 0), (0, LANE - w_router.shape[1])))
    vmem = (2 * _nbytes((bm, d), jnp.float32) + 2 * _nbytes((d, LANE), jnp.float32)
            + 6 * _nbytes((bm, LANE), jnp.float32) + _nbytes((bm, bm), jnp.float32))
    tile = pl.BlockSpec((bm, LANE), lambda i: (i, 0))
    return pl.pallas_call(
        _router_body,
        grid=(rows // bm,),
        in_specs=[pl.BlockSpec((bm, d), lambda i: (i, 0)),
                  pl.BlockSpec((1, d), lambda i: (0, 0)),
                  pl.BlockSpec((d, LANE), lambda i: (0, 0))],
        out_specs=[tile, tile, tile, pl.BlockSpec((1, LANE), lambda i: (0, 0))],
        out_shape=[jax.ShapeDtypeStruct((rows, LANE), jnp.int32),
                   jax.ShapeDtypeStruct((rows, LANE), jnp.int32),
                   jax.ShapeDtypeStruct((rows, LANE), jnp.float32),
                   jax.ShapeDtypeStruct((1, LANE), jnp.int32)],
        scratch_shapes=[pltpu.VMEM((1, LANE), jnp.float32)],
        compiler_params=_params(("arbitrary",), vmem),
        name="router",
    )(x, g.reshape(1, d).astype(jnp.float32), w_pad)


def _moe_plan(eid, rank, counts, bm, n_tiles):
    cnt = counts[0, :N_EXPERTS]
    padded = (cnt + bm - 1) // bm * bm
    ends = jnp.cumsum(padded)
    starts = ends - padded
    e_sel = eid[:, :TOP_K]
    start_sel = jnp.zeros_like(e_sel)
    for e in range(N_EXPERTS):
        start_sel = jnp.where(e_sel == e, starts[e], start_sel)
    slot = (start_sel + rank[:, :TOP_K]).astype(jnp.int32)
    tile_start = jnp.arange(n_tiles, dtype=jnp.int32) * bm
    tile_expert = jnp.minimum(jnp.sum(tile_start[:, None] >= ends[None, :], axis=1), N_EXPERTS - 1)
    n_used = (ends[-1] // bm).reshape(1)
    return slot, tile_expert.astype(jnp.int32), n_used.astype(jnp.int32)


def _pack_bf16_pairs(y):
    half = y.shape[1] // 2
    bits = lax.bitcast_convert_type(y.astype(jnp.bfloat16).astype(jnp.float32), jnp.uint32)
    return bits[:, half:] | (bits[:, :half] >> 16)


def _unpack_bf16_pairs(words):
    lo = lax.bitcast_convert_type(words << 16, jnp.float32).astype(jnp.bfloat16)
    hi = lax.bitcast_convert_type(words & jnp.uint32(0xFFFF0000), jnp.float32).astype(jnp.bfloat16)
    return lo, hi


def _rmsnorm_packed_body(x_ref, g_ref, o_ref):
    x = x_ref[...]
    y = x * lax.rsqrt(jnp.mean(x * x, axis=-1, keepdims=True) + RMS_EPS) * g_ref[...]
    o_ref[...] = _pack_bf16_pairs(y)


def _rmsnorm_packed(x, g):
    rows, d = x.shape
    bm = _tile(rows, NORM_BM)
    vmem = 2 * _nbytes((bm, d), x.dtype) + 2 * _nbytes((bm, d // 2), jnp.uint32)
    return pl.pallas_call(
        _rmsnorm_packed_body,
        grid=(rows // bm,),
        in_specs=[pl.BlockSpec((bm, d), lambda i: (i, 0)), pl.BlockSpec((1, d), lambda i: (0, 0))],
        out_specs=pl.BlockSpec((bm, d // 2), lambda i: (i, 0)),
        out_shape=jax.ShapeDtypeStruct((rows, d // 2), jnp.uint32),
        compiler_params=_params(("parallel",), vmem),
        name="norm_ffn_packed",
    )(x, g.reshape(1, d).astype(jnp.float32))


def _scatter_body(slot_ref, rows_hbm, init_hbm, out_hbm, sem):
    del init_hbm
    bt = slot_ref.shape[1] // TOP_K
    tok0 = pl.program_id(0) * bt

    def row_copy(r, c):
        return pltpu.make_async_copy(rows_hbm.at[pl.ds(tok0 + r, 1)],
                                     out_hbm.at[pl.ds(slot_ref[0, r * TOP_K + c], 1)], sem)

    def start(r, _):
        for c in range(TOP_K):
            row_copy(r, c).start()
        return 0

    def wait(r, _):
        for c in range(TOP_K):
            row_copy(r, c).wait()
        return 0

    lax.fori_loop(0, bt, start, 0)
    lax.fori_loop(0, bt, wait, 0)


def _moe_scatter(rows, slot, n_sorted):
    t, width = rows.shape
    bt = _tile(t, MOE_ROW_TILE)
    slot_blk = slot.reshape(t // bt, 1, bt * TOP_K)
    init = jnp.zeros((n_sorted, width), rows.dtype)
    return pl.pallas_call(
        _scatter_body,
        grid=(t // bt,),
        in_specs=[pl.BlockSpec((None, 1, bt * TOP_K), lambda i: (i, 0, 0), memory_space=pltpu.SMEM),
                  pl.BlockSpec(memory_space=pl.ANY),
                  pl.BlockSpec(memory_space=pl.ANY)],
        out_specs=pl.BlockSpec(memory_space=pl.ANY),
        out_shape=jax.ShapeDtypeStruct((n_sorted, width), rows.dtype),
        scratch_shapes=[pltpu.SemaphoreType.DMA(())],
        input_output_aliases={2: 0},
        compiler_params=_params(("arbitrary",), 0),
        name="moe_scatter",
    )(slot_blk, rows, init)


def _grouped_body(packed, swiglu, te_ref, nu_ref, a_ref, *refs):
    del te_ref
    w_refs, o_ref = refs[:-1], refs[-1]
    i = pl.program_id(1)

    @pl.when(i < nu_ref[0])
    def _():
        if packed:
            lo, hi = _unpack_bf16_pairs(a_ref[...])
            half = lo.shape[1]
            prods = [jnp.dot(lo, w[pl.ds(0, half), :], preferred_element_type=jnp.float32)
                     + jnp.dot(hi, w[pl.ds(half, half), :], preferred_element_type=jnp.float32)
                     for w in w_refs]
        else:
            a = a_ref[...]
            prods = [jnp.dot(a, w[...], preferred_element_type=jnp.float32) for w in w_refs]
        o_ref[...] = (_ep_swiglu(prods) if swiglu else prods[0]).astype(o_ref.dtype)

    @pl.when(i >= nu_ref[0])
    def _():
        o_ref[...] = jnp.zeros_like(o_ref)


def _grouped_mm(name, a, weights, tile_expert, n_used, *, bm, packed, swiglu, out_dtype):
    m = a.shape[0]
    _, kdim, n = weights[0].shape
    bn = _tile(n, MM_BN_MULTI if len(weights) > 1 else MM_BN)
    w_spec = pl.BlockSpec((None, kdim, bn), lambda j, i, te, nu: (te[i], 0, j))
    vmem = (2 * _nbytes((bm, a.shape[1]), a.dtype) + 2 * len(weights) * _nbytes((kdim, bn), weights[0].dtype)
            + 2 * _nbytes((bm, bn), out_dtype) + (_nbytes((bm, kdim), jnp.bfloat16) if packed else 0))
    grid_spec = pltpu.PrefetchScalarGridSpec(
        num_scalar_prefetch=2,
        grid=(n // bn, m // bm),
        in_specs=[pl.BlockSpec((bm, a.shape[1]), lambda j, i, te, nu: (i, 0))] + [w_spec] * len(weights),
        out_specs=pl.BlockSpec((bm, bn), lambda j, i, te, nu: (i, j)),
    )
    return pl.pallas_call(
        functools.partial(_grouped_body, packed, swiglu),
        grid_spec=grid_spec,
        out_shape=jax.ShapeDtypeStruct((m, n), out_dtype),
        compiler_params=_params(("parallel", "arbitrary"), vmem),
        name=name,
    )(tile_expert, n_used, a, *weights)


def _combine_body(with_norm, slot_ref, slot_next_ref, x_ref, wts_ref, *refs):
    if with_norm:
        g_ref, y_hbm, o_ref, buf, sem = refs
    else:
        y_hbm, o_ref, buf, sem = refs
    i = pl.program_id(0)
    n_steps = pl.num_programs(0)
    bt = x_ref.shape[0]

    def row_copy(slots, half, r, c):
        return pltpu.make_async_copy(y_hbm.at[pl.ds(slots[0, r * TOP_K + c], 1)],
                                     buf.at[half, c, pl.ds(r, 1)], sem.at[half])

    def start_all(slots, half):
        def body(r, _):
            for c in range(TOP_K):
                row_copy(slots, half, r, c).start()
            return 0
        lax.fori_loop(0, bt, body, 0)

    def wait_all(slots, half):
        def body(r, _):
            for c in range(TOP_K):
                row_copy(slots, half, r, c).wait()
            return 0
        lax.fori_loop(0, bt, body, 0)

    cur = i % 2

    @pl.when(i == 0)
    def _():
        start_all(slot_ref, 0)

    @pl.when(i + 1 < n_steps)
    def _():
        start_all(slot_next_ref, 1 - cur)

    wait_all(slot_ref, cur)
    wts = wts_ref[...]
    lane = lax.broadcasted_iota(jnp.int32, wts.shape, 1)
    out = x_ref[...]
    for c in range(TOP_K):
        w_c = jnp.sum(jnp.where(lane == c, wts, 0.0), axis=1, keepdims=True)
        out = out + w_c * buf[cur, c]
    if with_norm:
        out = out * lax.rsqrt(jnp.mean(out * out, axis=-1, keepdims=True) + RMS_EPS) * g_ref[...]
    o_ref[...] = out.astype(o_ref.dtype)


def _moe_combine(x, y_sorted, slot, wts, g_final=None):
    t, d = x.shape
    bt = _tile(t, MOE_COMBINE_TILE)
    n_steps = t // bt
    slot_blk = slot.reshape(n_steps, 1, bt * TOP_K)
    slot_spec = lambda f: pl.BlockSpec((None, 1, bt * TOP_K), lambda i: (f(i), 0, 0), memory_space=pltpu.SMEM)
    in_specs = [slot_spec(lambda i: i), slot_spec(lambda i: jnp.minimum(i + 1, n_steps - 1)),
                pl.BlockSpec((bt, d), lambda i: (i, 0)), pl.BlockSpec((bt, LANE), lambda i: (i, 0))]
    args = [slot_blk, slot_blk, x, wts]
    if g_final is not None:
        in_specs.append(pl.BlockSpec((1, d), lambda i: (0, 0)))
        args.append(g_final.reshape(1, d).astype(jnp.float32))
    in_specs.append(pl.BlockSpec(memory_space=pl.ANY))
    args.append(y_sorted)
    vmem = (4 * _nbytes((bt, d), jnp.float32) + 2 * _nbytes((bt, LANE), jnp.float32)
            + 2 * TOP_K * _nbytes((bt, d), y_sorted.dtype))
    return pl.pallas_call(
        functools.partial(_combine_body, g_final is not None),
        grid=(n_steps,),
        in_specs=in_specs,
        out_specs=pl.BlockSpec((bt, d), lambda i: (i, 0)),
        out_shape=jax.ShapeDtypeStruct((t, d), x.dtype),
        scratch_shapes=[pltpu.VMEM((2, TOP_K, bt, d), y_sorted.dtype), pltpu.SemaphoreType.DMA((2,))],
        compiler_params=_params(("arbitrary",), vmem),
        name="moe_combine",
    )(*args)


def _moe_ffn(x2, g_ffn, w_router, w_gate, w_up, w_down, g_final=None):
    t, d = x2.shape
    bm = _tile(t, MOE_ROW_TILE)
    n_sorted = TOP_K * t + N_EXPERTS * bm
    eid, rank, wts, counts = _router(x2, g_ffn, w_router)
    slot, tile_expert, n_used = _moe_plan(eid, rank, counts, bm, n_sorted // bm)
    xn_sorted = _moe_scatter(_rmsnorm_packed(x2, g_ffn), slot, n_sorted)
    h = _grouped_mm("moe_up", xn_sorted, [_bf16(w_gate), _bf16(w_up)], tile_expert, n_used,
                    bm=bm, packed=True, swiglu=True, out_dtype=jnp.bfloat16)
    y = _grouped_mm("moe_down", h, [_bf16(w_down)], tile_expert, n_used,
                    bm=bm, packed=False, swiglu=False, out_dtype=jnp.float32)
    return _moe_combine(x2, y, slot, wts, g_final)


def _swiglu_up(name, xn, w_gate, w_up):
    m, d = xn.shape
    f = w_gate.shape[1]
    bm, bn = _tile(m, MM_BM), _tile(f, MM_BN_MULTI)
    return _mm(name, [(xn, _a_spec(bm, d))], [(w_gate, _w_spec(d, bn)), (w_up, _w_spec(d, bn))], [],
               [(0, 0), (0, 1)], _ep_swiglu, m=m, n=f, bm=bm, bn=bn, nk=1, out_dtype=jnp.bfloat16)


def _bf16(w):
    return w.astype(jnp.bfloat16)


def _mixer(x2, batch, seq, g_mix, w_in, b_forget, b_gate, g_q, g_kv, w_uq, w_ukv, sinks, w_branch,
           w_out, tables):
    d = x2.shape[1]
    fox_w = FOX_HEADS * FOX_HEAD_DIM
    swa_w = SWA_Q_HEADS * SWA_HEAD_DIM
    swa_kv_w = SWA_KV_HEADS * SWA_HEAD_DIM
    mla_w = MLA_HEADS * MLA_V_DIM
    o_lat = 0
    o_fox = Q_LORA + KV_LORA + ROPE_DIM
    o_flog = o_fox + 3 * fox_w
    o_swa = o_flog + FOX_HEADS
    o_gate = o_swa + swa_w + 2 * swa_kv_w

    lat_w = Q_LORA + KV_LORA + LANE
    lat_pad = -(lat_w) % 512
    w_lat = _bf16(jnp.pad(w_in[:, o_lat:o_fox], ((0, 0), (0, LANE - ROPE_DIM + lat_pad))))
    w_fox = _bf16(w_in[:, o_fox:o_flog])
    w_flog = _bf16(jnp.pad(w_in[:, o_flog:o_swa], ((0, 0), (0, LANE - FOX_HEADS))))
    w_swa = _bf16(w_in[:, o_swa:o_gate])
    w_gate = _bf16(w_in[:, o_gate:])

    xn = _rmsnorm(x2, g_mix, jnp.bfloat16, name="norm_mix")
    lat = _linear("in_latent", xn, w_lat, jnp.float32, bn=512)
    fox_bn, swa_bn = _tile(3 * fox_w, MM_BN), _tile(swa_w + 2 * swa_kv_w, MM_BN)
    assert fox_w % fox_bn == 0 and swa_w % swa_bn == 0
    fqkv = _linear("in_fox", xn, w_fox, jnp.bfloat16, bn=fox_bn, epilogue=functools.partial(
        _ep_scale_leading, fox_w // fox_bn, FOX_HEAD_DIM ** -0.5 * LOG2E))
    flog = _linear("in_forget", xn, w_flog, jnp.float32)
    sqkv = _linear("in_swa", xn, w_swa, jnp.bfloat16, bn=swa_bn, epilogue=functools.partial(
        _ep_scale_leading, swa_w // swa_bn, SWA_HEAD_DIM ** -0.5 * LOG2E))
    gates = _linear("in_gate", xn, w_gate, jnp.bfloat16, epilogue=_ep_sigmoid_bias,
                    extras=[(b_gate.reshape(1, -1).astype(jnp.float32), "row")])

    cq_n = _rmsnorm(lat, g_q, jnp.bfloat16, col_block=0, width=Q_LORA, name="norm_q")
    ckv_n = _rmsnorm(lat, g_kv, jnp.bfloat16, col_block=Q_LORA // KV_LORA, width=KV_LORA, name="norm_kv")
    w_uq_p = w_uq.reshape(Q_LORA, MLA_HEADS, NOPE_DIM + ROPE_DIM)
    w_uq_p = _bf16(jnp.pad(w_uq_p, ((0, 0), (0, 0), (0, ROPE_DIM))).reshape(Q_LORA, MLA_HEADS * MLA_QK))
    w_ukv_h = w_ukv.reshape(KV_LORA, MLA_HEADS, NOPE_DIM + MLA_V_DIM)
    w_k = _bf16(w_ukv_h[:, :, :NOPE_DIM].reshape(KV_LORA, MLA_HEADS * NOPE_DIM))
    w_v = _bf16(w_ukv_h[:, :, NOPE_DIM:].reshape(KV_LORA, MLA_HEADS * MLA_V_DIM))
    q_mla = _mla_q(cq_n, w_uq_p, tables, seq)
    k_mla, v_mla = _mla_kv(ckv_n, w_k, w_v, lat, (Q_LORA + KV_LORA) // LANE, tables, seq)
    o_mla = _flash("attn_mla", q_mla, k_mla, v_mla, q_col0=0, k_col0=0, v_col0=0, dk=MLA_QK,
                   dv=MLA_V_DIM, heads=MLA_HEADS, batch=batch, seq=seq)

    b_pad = jnp.pad(b_forget.astype(jnp.float32), (0, LANE - FOX_HEADS)).reshape(1, LANE)
    cum = _forget_cumsum(flog, b_pad, batch, seq)
    bk = _tile(seq, FLASH_BK)
    cum_row = cum[:, :FOX_HEADS].reshape(batch, seq, FOX_HEADS).transpose(0, 2, 1)
    cum_row = cum_row.reshape(batch * FOX_HEADS, seq // bk, 1, bk)
    o_foxa = _flash("attn_fox", fqkv, fqkv, fqkv, q_col0=0, k_col0=FOX_HEADS, v_col0=2 * FOX_HEADS,
                    dk=FOX_HEAD_DIM, dv=FOX_HEAD_DIM, heads=FOX_HEADS, batch=batch, seq=seq,
                    cum_col=cum, cum_row=cum_row)

    o_swa_a = _swa(sqkv, sinks, batch, seq)

    m = x2.shape[0]
    bm, bn = _tile(m, MM_BM), _tile(d, MM_BN_MULTI)
    nj = d // bn
    wb = [_bf16(w_branch[:mla_w]), _bf16(w_branch[mla_w:mla_w + fox_w]), _bf16(w_branch[mla_w + fox_w:])]
    outs = [o_mla, o_foxa, o_swa_a]
    merged = _mm("branch_merge",
                 [(o, _a_spec(bm, o.shape[1])) for o in outs],
                 [(w, _w_spec(w.shape[0], bn)) for w in wb],
                 [(gates, _tile_spec(bm, bn, col_off=b * nj)) for b in range(N_BRANCHES)],
                 [(0, 0), (1, 1), (2, 2)], _ep_merge,
                 m=m, n=d, bm=bm, bn=bn, nk=1, out_dtype=jnp.bfloat16)
    return _linear("out_proj", merged, _bf16(w_out), jnp.float32, epilogue=_ep_residual,
                   extras=[(x2, "tile")])


def kernel(x, g_mix_norm, w_in, b_forget, b_gate, g_q_norm, g_kv_norm, w_uq, w_ukv, sinks, w_branch,
           w_out, g_ffn_norm, w_dense_gate, w_dense_up, w_dense_down, w_router, w_exp_gate, w_exp_up,
           w_exp_down, g_final):
    batch, seq, d = x.shape
    depth = w_in.shape[0]
    x2 = x.reshape(batch * seq, d)
    tables = _rope_tables(seq)
    for l in range(depth):
        x2 = _mixer(x2, batch, seq, g_mix_norm[l], w_in[l], b_forget[l], b_gate[l], g_q_norm[l],
                    g_kv_norm[l], w_uq[l], w_ukv[l], sinks[l], w_branch[l], w_out[l], tables)
        j = l // 2
        last = l == depth - 1
        if l % 2 == 0:
            xn = _rmsnorm(x2, g_ffn_norm[l], jnp.bfloat16, name="norm_ffn")
            h = _swiglu_up("dense_up", xn, _bf16(w_dense_gate[j]), _bf16(w_dense_up[j]))
            x2 = _linear("ffn_down", h, _bf16(w_dense_down[j]), jnp.float32, epilogue=_ep_residual,
                         extras=[(x2, "tile")])
            if last:
                x2 = _rmsnorm(x2, g_final, x.dtype, name="norm_final")
        else:
            x2 = _moe_ffn(x2, g_ffn_norm[l], w_router[j], w_exp_gate[j], w_exp_up[j], w_exp_down[j],
                          g_final if last else None)
    return x2.reshape(batch, seq, d)
```

```python
import functools

import jax
import jax.numpy as jnp
from jax import lax
from jax.experimental import pallas as pl
from jax.experimental.pallas import tpu as pltpu

RMS_EPS = 1e-6
MLA_HEADS = 16
Q_LORA = 1536
KV_LORA = 512
NOPE_DIM = 128
ROPE_DIM = 64
MLA_V_DIM = 128
ROPE_THETA = 10000.0
FOX_HEADS = 16
FOX_HEAD_DIM = 128
SWA_Q_HEADS = 32
SWA_KV_HEADS = 4
SWA_HEAD_DIM = 64
WINDOW = 128
N_BRANCHES = 3
N_EXPERTS = 8
TOP_K = 2

LANE = 128
V7X_VMEM_BYTES = 64 * 1024 * 1024
V7X_VMEM_BUDGET = 56 * 1024 * 1024
COMPILER_SCRATCH_BYTES = 12 * 1024 * 1024

MM_BM = 1024
MM_BN = 1024
MM_BK = 2048
MM_FULL_K = 4096
MM_BN_MULTI = 512
NORM_BM = 512
FLASH_BQ = 512
FLASH_BK = 512
FLASH_HEADS_PER_STEP = 2
CUM_BLOCK = 512
MLA_KV_BM = 512
ROUTER_BM = 512
MOE_ROW_TILE = 512
MOE_COMBINE_TILE = 256
DMA_ISSUE_UNROLL = 8

MLA_QK = NOPE_DIM + 2 * ROPE_DIM
LOG2E = 1.4426950408889634


def _params(semantics, vmem_bytes):
    limit = min(V7X_VMEM_BUDGET, vmem_bytes + COMPILER_SCRATCH_BYTES)
    return pltpu.CompilerParams(dimension_semantics=semantics, vmem_limit_bytes=int(limit))


def _nbytes(shape, dtype):
    n = 1
    for s in shape:
        n *= s
    return n * jnp.dtype(dtype).itemsize


def _tile(full, want):
    t = min(full, want)
    while full % t:
        t //= 2
    return t


def _rmsnorm_body(x_ref, g_ref, o_ref):
    x = x_ref[...].astype(jnp.float32)
    y = x * lax.rsqrt(jnp.mean(x * x, axis=-1, keepdims=True) + RMS_EPS)
    o_ref[...] = (y * g_ref[...]).astype(o_ref.dtype)


def _rmsnorm(x, g, out_dtype, *, col_block=0, width=None, name="rmsnorm"):
    rows = x.shape[0]
    width = x.shape[1] if width is None else width
    bm = _tile(rows, NORM_BM)
    vmem = 2 * _nbytes((bm, width), x.dtype) + 2 * _nbytes((bm, width), out_dtype)
    return pl.pallas_call(
        _rmsnorm_body,
        grid=(rows // bm,),
        in_specs=[pl.BlockSpec((bm, width), lambda i: (i, col_block)),
                  pl.BlockSpec((1, width), lambda i: (0, 0))],
        out_specs=pl.BlockSpec((bm, width), lambda i: (i, 0)),
        out_shape=jax.ShapeDtypeStruct((rows, width), out_dtype),
        compiler_params=_params(("parallel",), vmem),
        name=name,
    )(x, g.reshape(1, width).astype(jnp.float32))


def _mm_body(n_a, n_w, n_x, dots, nk, epilogue, *refs):
    a = refs[:n_a]
    w = refs[n_a:n_a + n_w]
    xs = refs[n_a + n_w:n_a + n_w + n_x]
    o = refs[n_a + n_w + n_x]
    accs = refs[n_a + n_w + n_x + 1:]

    def products():
        return [jnp.dot(a[ai][...], w[wi][...], preferred_element_type=jnp.float32)
                for ai, wi in dots]

    def finish(vals):
        o[...] = epilogue(vals, *[x[...] for x in xs]).astype(o.dtype)

    if nk == 1:
        finish(products())
        return

    k = pl.program_id(2)

    @pl.when(k == 0)
    def _():
        for acc, p in zip(accs, products()):
            acc[...] = p

    @pl.when((k > 0) & (k < nk - 1))
    def _():
        for acc, p in zip(accs, products()):
            acc[...] += p

    @pl.when(k == nk - 1)
    def _():
        finish([acc[...] + p for acc, p in zip(accs, products())])


def _mm(name, a_ops, w_ops, x_ops, dots, epilogue, *, m, n, bm, bn, nk, out_dtype):
    ops = list(a_ops) + list(w_ops) + list(x_ops)
    vmem = 2 * _nbytes((bm, bn), out_dtype)
    for arr, spec in ops:
        blk = [d for d in spec.block_shape if d is not None]
        vmem += 2 * _nbytes(blk, arr.dtype)
    scratch = []
    if nk > 1:
        scratch = [pltpu.VMEM((bm, bn), jnp.float32) for _ in dots]
        vmem += len(dots) * _nbytes((bm, bn), jnp.float32)
    body = functools.partial(_mm_body, len(a_ops), len(w_ops), len(x_ops), tuple(dots), nk, epilogue)
    return pl.pallas_call(
        body,
        grid=(m // bm, n // bn, nk),
        in_specs=[spec for _, spec in ops],
        out_specs=pl.BlockSpec((bm, bn), lambda i, j, k: (i, j)),
        out_shape=jax.ShapeDtypeStruct((m, n), out_dtype),
        scratch_shapes=scratch,
        compiler_params=_params(("parallel", "parallel", "arbitrary"), vmem),
        name=name,
    )(*[arr for arr, _ in ops])


def _a_spec(bm, bk):
    return pl.BlockSpec((bm, bk), lambda i, j, k: (i, k))


def _w_spec(bk, bn):
    return pl.BlockSpec((bk, bn), lambda i, j, k: (k, j))


def _tile_spec(bm, bn, col_off=0):
    return pl.BlockSpec((bm, bn), lambda i, j, k: (i, j + col_off))


def _row_spec(bn, col_off=0):
    return pl.BlockSpec((1, bn), lambda i, j, k: (0, j + col_off))


def _ep_plain(vals):
    return vals[0]


def _ep_residual(vals, res):
    return res + vals[0]


def _ep_sigmoid_bias(vals, bias):
    return jax.nn.sigmoid(vals[0] + bias)


def _ep_scale_leading(n_blocks, factor, vals):
    return vals[0] * jnp.where(pl.program_id(1) < n_blocks, factor, 1.0)


def _ep_swiglu(vals):
    return jax.nn.silu(vals[0]) * vals[1]


def _ep_merge(vals, g0, g1, g2):
    return (g0.astype(jnp.float32) * vals[0] + g1.astype(jnp.float32) * vals[1]
            + g2.astype(jnp.float32) * vals[2])


def _linear(name, a, w, out_dtype, *, epilogue=_ep_plain, extras=(), bn=MM_BN):
    m, kdim = a.shape
    n = w.shape[1]
    bm, bn, bk = _tile(m, MM_BM), _tile(n, bn), _tile(kdim, MM_BK if kdim > MM_FULL_K else kdim)
    x_ops = []
    for arr, kind in extras:
        x_ops.append((arr, _tile_spec(bm, bn) if kind == "tile" else _row_spec(bn)))
    return _mm(name, [(a, _a_spec(bm, bk))], [(w, _w_spec(bk, bn))], x_ops, [(0, 0)], epilogue,
               m=m, n=n, bm=bm, bn=bn, nk=kdim // bk, out_dtype=out_dtype)


def _rope_tile(t, cos_t, sin_a, sin_b):
    half = ROPE_DIM // 2
    return (t * cos_t + pltpu.roll(t, LANE - half, axis=1) * sin_a
            + pltpu.roll(t, half, axis=1) * sin_b)


def _rope_tables(seq):
    half = ROPE_DIM // 2
    inv = ROPE_THETA ** (-jnp.arange(half, dtype=jnp.float32) / half)
    ang = jnp.arange(seq, dtype=jnp.float32)[:, None] * inv[None, :]
    cos, sin = jnp.cos(ang), jnp.sin(ang)
    z = jnp.zeros_like(cos)
    pad = jnp.zeros((seq, LANE - 2 * half), jnp.float32)
    cos_t = jnp.concatenate([cos, cos, pad], axis=1)
    sin_a = jnp.concatenate([-sin, z, pad], axis=1)
    sin_b = jnp.concatenate([z, sin, pad], axis=1)
    return cos_t, sin_a, sin_b


def _ep_mla_q(vals, cos_t, sin_a, sin_b):
    acc = vals[0] * ((NOPE_DIM + ROPE_DIM) ** -0.5 * LOG2E)
    pieces = []
    for h in range(acc.shape[1] // MLA_QK):
        lo = h * MLA_QK
        pieces.append(acc[:, lo:lo + NOPE_DIM])
        pieces.append(_rope_tile(acc[:, lo + NOPE_DIM:lo + MLA_QK], cos_t, sin_a, sin_b))
    return jnp.concatenate(pieces, axis=1)


def _mla_q(cq_n, w_uq_p, tables, seq):
    m, kdim = cq_n.shape
    n = w_uq_p.shape[1]
    bm = _tile(seq, MM_BM)
    bn = _tile(n, MM_BN)
    nseq = seq // bm
    tab_spec = pl.BlockSpec((bm, LANE), lambda i, j, k: (i % nseq, 0))
    return _mm("mla_q", [(cq_n, _a_spec(bm, kdim))], [(w_uq_p, _w_spec(kdim, bn))],
               [(t, tab_spec) for t in tables], [(0, 0)], _ep_mla_q,
               m=m, n=n, bm=bm, bn=bn, nk=1, out_dtype=jnp.bfloat16)


def _mla_kv_body(heads, a_ref, wk_ref, wv_ref, kr_ref, cos_ref, sina_ref, sinb_ref, k_ref, v_ref):
    a = a_ref[...]
    kn = jnp.dot(a, wk_ref[...], preferred_element_type=jnp.float32)
    v_ref[...] = jnp.dot(a, wv_ref[...], preferred_element_type=jnp.float32).astype(v_ref.dtype)
    kr = _rope_tile(kr_ref[...], cos_ref[...], sina_ref[...], sinb_ref[...]).astype(k_ref.dtype)
    for h in range(heads):
        k_ref[:, h * MLA_QK:h * MLA_QK + NOPE_DIM] = kn[:, h * NOPE_DIM:(h + 1) * NOPE_DIM].astype(k_ref.dtype)
        k_ref[:, h * MLA_QK + NOPE_DIM:(h + 1) * MLA_QK] = kr


def _mla_kv(ckv_n, w_k, w_v, lat, kr_col_block, tables, seq):
    m, kdim = ckv_n.shape
    heads = w_k.shape[1] // NOPE_DIM
    bm = _tile(seq, MLA_KV_BM)
    nseq = seq // bm
    tab_spec = pl.BlockSpec((bm, LANE), lambda i: (i % nseq, 0))
    nk_out, nv_out = heads * MLA_QK, heads * MLA_V_DIM
    vmem = (2 * _nbytes((bm, kdim), ckv_n.dtype) + 2 * _nbytes(w_k.shape, w_k.dtype)
            + 2 * _nbytes(w_v.shape, w_v.dtype) + 8 * _nbytes((bm, LANE), jnp.float32)
            + 2 * _nbytes((bm, nk_out), jnp.bfloat16) + 2 * _nbytes((bm, nv_out), jnp.bfloat16))
    return pl.pallas_call(
        functools.partial(_mla_kv_body, heads),
        grid=(m // bm,),
        in_specs=[pl.BlockSpec((bm, kdim), lambda i: (i, 0)),
                  pl.BlockSpec(w_k.shape, lambda i: (0, 0)),
                  pl.BlockSpec(w_v.shape, lambda i: (0, 0)),
                  pl.BlockSpec((bm, LANE), lambda i: (i, kr_col_block)),
                  tab_spec, tab_spec, tab_spec],
        out_specs=[pl.BlockSpec((bm, nk_out), lambda i: (i, 0)),
                   pl.BlockSpec((bm, nv_out), lambda i: (i, 0))],
        out_shape=[jax.ShapeDtypeStruct((m, nk_out), jnp.bfloat16),
                   jax.ShapeDtypeStruct((m, nv_out), jnp.bfloat16)],
        compiler_params=_params(("parallel",), vmem),
        name="mla_kv",
    )(ckv_n, w_k, w_v, lat, *tables)


def _split_bf16(x):
    hi = x.astype(jnp.bfloat16)
    r = x - hi.astype(jnp.float32)
    mid = r.astype(jnp.bfloat16)
    lo = (r - mid.astype(jnp.float32)).astype(jnp.bfloat16)
    return hi, mid, lo


def _cum_body(logit_ref, b_ref, o_ref, carry_ref):
    @pl.when(pl.program_id(1) == 0)
    def _():
        carry_ref[...] = jnp.zeros_like(carry_ref)

    z = logit_ref[...] + b_ref[...]
    log_f = (jnp.minimum(z, 0.0) - jnp.log1p(jnp.exp(-jnp.abs(z)))) * LOG2E
    n = log_f.shape[0]
    tri = (lax.broadcasted_iota(jnp.int32, (n, n), 0)
           >= lax.broadcasted_iota(jnp.int32, (n, n), 1)).astype(jnp.bfloat16)
    cum = carry_ref[...]
    for part in _split_bf16(log_f):
        cum = cum + jnp.dot(tri, part, preferred_element_type=jnp.float32)
    o_ref[...] = cum
    carry_ref[...] = cum[n - 1:n, :]


def _forget_cumsum(logit, b_pad, batch, seq):
    bm = _tile(seq, CUM_BLOCK)
    ns = seq // bm
    vmem = 4 * _nbytes((bm, LANE), jnp.float32) + _nbytes((bm, bm), jnp.float32)
    return pl.pallas_call(
        _cum_body,
        grid=(batch, ns),
        in_specs=[pl.BlockSpec((bm, LANE), lambda b, s: (b * ns + s, 0)),
                  pl.BlockSpec((1, LANE), lambda b, s: (0, 0))],
        out_specs=pl.BlockSpec((bm, LANE), lambda b, s: (b * ns + s, 0)),
        out_shape=jax.ShapeDtypeStruct(logit.shape, jnp.float32),
        scratch_shapes=[pltpu.VMEM((1, LANE), jnp.float32)],
        compiler_params=_params(("parallel", "arbitrary"), vmem),
        name="forget_cumsum",
    )(logit, b_pad)


def _flash_body(bq, bk, hp, dk, dv, with_cum, *refs):
    if with_cum:
        q_ref, k_ref, v_ref, cc_ref, cr_ref, o_ref = refs
    else:
        q_ref, k_ref, v_ref, o_ref = refs
    i = pl.program_id(2)
    qs = [q_ref[:, h * dk:(h + 1) * dk] for h in range(hp)]
    if with_cum:
        head0 = pl.program_id(1) * hp
        cc = cc_ref[...]
        lane = lax.broadcasted_iota(jnp.int32, cc.shape, 1)
        cum_q = [jnp.sum(jnp.where(lane == head0 + h, cc, 0.0), axis=1, keepdims=True)
                 for h in range(hp)]

    def step(j, carry, diagonal):
        start = pl.multiple_of(j * bk, bk)
        if diagonal:
            causal = (lax.broadcasted_iota(jnp.int32, (bq, bk), 0)
                      >= lax.broadcasted_iota(jnp.int32, (bq, bk), 1))
        out = []
        for h in range(hp):
            m, l, acc = carry[h]
            k = k_ref[pl.ds(start, bk), h * dk:(h + 1) * dk]
            v = v_ref[pl.ds(start, bk), h * dv:(h + 1) * dv]
            s = lax.dot_general(qs[h], k, (((1,), (1,)), ((), ())), preferred_element_type=jnp.float32)
            if with_cum:
                s = s - cr_ref[h, j]
            if diagonal:
                s = jnp.where(causal, s, -jnp.inf)
            row_max = jnp.max(s, axis=1, keepdims=True)
            if with_cum:
                m_new = jnp.maximum(m, row_max + cum_q[h])
                shift = m_new - cum_q[h]
            else:
                m_new = jnp.maximum(m, row_max)
                shift = m_new
            alpha = jnp.exp2(m - m_new)
            p = jnp.exp2(s - shift)
            l = alpha * l + jnp.sum(p, axis=1, keepdims=True)
            acc = alpha * acc + jnp.dot(p.astype(v.dtype), v, preferred_element_type=jnp.float32)
            out.append((m_new, l, acc))
        return tuple(out)

    init = tuple((jnp.full((bq, 1), -jnp.inf, jnp.float32), jnp.zeros((bq, 1), jnp.float32),
                  jnp.zeros((bq, dv), jnp.float32)) for _ in range(hp))
    carry = lax.fori_loop(0, i, lambda j, c: step(j, c, False), init)
    final = step(i, carry, True)
    for h in range(hp):
        _, l, acc = final[h]
        o_ref[:, h * dv:(h + 1) * dv] = (acc / l).astype(o_ref.dtype)


def _flash(name, q_arr, k_arr, v_arr, *, q_col0, k_col0, v_col0, dk, dv, heads, batch, seq,
           cum_col=None, cum_row=None):
    bq = _tile(seq, FLASH_BQ)
    bk = bq
    nq = seq // bq
    hp = FLASH_HEADS_PER_STEP
    assert heads % hp == 0 and q_col0 % hp == 0 and k_col0 % hp == 0 and v_col0 % hp == 0
    with_cum = cum_col is not None
    in_specs = [pl.BlockSpec((bq, hp * dk), lambda b, g, i: (b * nq + i, q_col0 // hp + g)),
                pl.BlockSpec((seq, hp * dk), lambda b, g, i: (b, k_col0 // hp + g)),
                pl.BlockSpec((seq, hp * dv), lambda b, g, i: (b, v_col0 // hp + g))]
    args = [q_arr, k_arr, v_arr]
    vmem = (2 * _nbytes((bq, hp * dk), q_arr.dtype) + 2 * _nbytes((seq, hp * dk), k_arr.dtype)
            + 2 * _nbytes((seq, hp * dv), v_arr.dtype) + 2 * _nbytes((bq, hp * dv), jnp.bfloat16))
    if with_cum:
        in_specs += [pl.BlockSpec((bq, LANE), lambda b, g, i: (b * nq + i, 0)),
                     pl.BlockSpec((hp, nq, 1, bk), lambda b, g, i: (b * (heads // hp) + g, 0, 0, 0))]
        args += [cum_col, cum_row]
        vmem += 2 * _nbytes((bq, LANE), jnp.float32) + 2 * _nbytes((hp, nq, 8, bk), jnp.float32)
    return pl.pallas_call(
        functools.partial(_flash_body, bq, bk, hp, dk, dv, with_cum),
        grid=(batch, heads // hp, nq),
        in_specs=in_specs,
        out_specs=pl.BlockSpec((bq, hp * dv), lambda b, g, i: (b * nq + i, g)),
        out_shape=jax.ShapeDtypeStruct((batch * seq, heads * dv), jnp.bfloat16),
        compiler_params=_params(("parallel", "parallel", "arbitrary"), vmem),
        name=name,
    )(*args)


def _block_diag_pair(slab, head_in_slab):
    d = SWA_HEAD_DIM
    lane = lax.broadcasted_iota(jnp.int32, slab.shape, 1)
    x32 = slab.astype(jnp.float32)
    if head_in_slab == 0:
        left = jnp.where(lane < d, x32, 0.0)
        right = pltpu.roll(left, d, axis=1)
    else:
        right = jnp.where(lane >= d, x32, 0.0)
        left = pltpu.roll(right, d, axis=1)
    return jnp.concatenate([left, right], axis=0).astype(slab.dtype)


def _swa_body(sink_ref, q_ref, kp_ref, kc_ref, vp_ref, vc_ref, o_ref):
    n = pl.program_id(1)
    blk = q_ref.shape[0]
    group = SWA_Q_HEADS // SWA_KV_HEADS
    d = SWA_HEAD_DIM
    keys = 2 * blk
    row = lax.broadcasted_iota(jnp.int32, (blk, keys), 0)
    col = lax.broadcasted_iota(jnp.int32, (blk, keys), 1)
    dist = row + blk - col
    valid = (dist >= 0) & (dist < WINDOW) & ((col >= blk) | (n > 0))
    dist_f = dist.astype(jnp.float32)
    out_lane = lax.broadcasted_iota(jnp.int32, (blk, 2 * d), 1)
    for g in range(SWA_KV_HEADS):
        slab = slice((g // 2) * 2 * d, (g // 2 + 1) * 2 * d)
        k2 = _block_diag_pair(jnp.concatenate([kp_ref[:, slab], kc_ref[:, slab]], axis=0), g % 2)
        v2 = _block_diag_pair(jnp.concatenate([vp_ref[:, slab], vc_ref[:, slab]], axis=0), g % 2)
        for u in range(0, group, 2):
            h = g * group + u
            slopes = [LOG2E * 2.0 ** (-8.0 * (h + c + 1) / SWA_Q_HEADS) for c in range(2)]
            sinks = [sink_ref[h + c] * LOG2E for c in range(2)]
            q2 = q_ref[:, h * d:(h + 2) * d]
            s = lax.dot_general(q2, k2, (((1,), (1,)), ((), ())), preferred_element_type=jnp.float32)
            ps, denoms = [], []
            for c in range(2):
                s_c = jnp.where(valid, s[:, c * keys:(c + 1) * keys] - slopes[c] * dist_f, -jnp.inf)
                m = jnp.maximum(jnp.max(s_c, axis=1, keepdims=True), sinks[c])
                p = jnp.exp2(s_c - m)
                ps.append(p)
                denoms.append(jnp.sum(p, axis=1, keepdims=True) + jnp.exp2(sinks[c] - m))
            p2 = jnp.concatenate(ps, axis=1).astype(v2.dtype)
            o = jnp.dot(p2, v2, preferred_element_type=jnp.float32)
            o = o / jnp.where(out_lane < d, denoms[0], denoms[1])
            o_ref[:, h * d:(h + 2) * d] = o.astype(o_ref.dtype)


def _swa(sqkv, sinks, batch, seq):
    blk = WINDOW
    nb = seq // blk
    qw = SWA_Q_HEADS * SWA_HEAD_DIM
    kvw = SWA_KV_HEADS * SWA_HEAD_DIM
    k_col, v_col = qw // kvw, qw // kvw + 1
    cur = lambda col: pl.BlockSpec((blk, kvw), lambda b, n: (b * nb + n, col))
    prev = lambda col: pl.BlockSpec((blk, kvw), lambda b, n: (b * nb + jnp.maximum(n - 1, 0), col))
    vmem = 4 * _nbytes((blk, qw), jnp.bfloat16) + 8 * _nbytes((blk, kvw), jnp.bfloat16)
    return pl.pallas_call(
        _swa_body,
        grid=(batch, nb),
        in_specs=[pl.BlockSpec(memory_space=pltpu.SMEM),
                  pl.BlockSpec((blk, qw), lambda b, n: (b * nb + n, 0)),
                  prev(k_col), cur(k_col), prev(v_col), cur(v_col)],
        out_specs=pl.BlockSpec((blk, qw), lambda b, n: (b * nb + n, 0)),
        out_shape=jax.ShapeDtypeStruct((batch * seq, qw), jnp.bfloat16),
        compiler_params=_params(("parallel", "parallel"), vmem),
        name="swa",
    )(sinks.astype(jnp.float32), sqkv, sqkv, sqkv, sqkv, sqkv)


def _router_body(x_ref, g_ref, w_ref, eid_ref, rank_ref, wts_ref, cnt_ref, base_ref):
    @pl.when(pl.program_id(0) == 0)
    def _():
        base_ref[...] = jnp.zeros_like(base_ref)

    x = x_ref[...]
    xn = x * lax.rsqrt(jnp.mean(x * x, axis=-1, keepdims=True) + RMS_EPS) * g_ref[...]
    logits = jnp.dot(xn, w_ref[...], preferred_element_type=jnp.float32,
                     precision=lax.Precision.HIGHEST)
    lane = lax.broadcasted_iota(jnp.int32, logits.shape, 1)
    logits = jnp.where(lane < N_EXPERTS, logits, -jnp.inf)
    top1 = jnp.max(logits, axis=1, keepdims=True)
    idx1 = jnp.min(jnp.where(logits == top1, lane, LANE), axis=1, keepdims=True)
    rest = jnp.where(lane == idx1, -jnp.inf, logits)
    top2 = jnp.max(rest, axis=1, keepdims=True)
    idx2 = jnp.min(jnp.where(rest == top2, lane, LANE), axis=1, keepdims=True)
    e2 = jnp.exp(top2 - top1)
    w1 = 1.0 / (1.0 + e2)
    w2 = e2 / (1.0 + e2)
    n = x.shape[0]
    hot = (lane == idx1) | (lane == idx2)
    earlier = (lax.broadcasted_iota(jnp.int32, (n, n), 0)
               > lax.broadcasted_iota(jnp.int32, (n, n), 1)).astype(jnp.bfloat16)
    before = base_ref[...] + jnp.dot(earlier, hot.astype(jnp.bfloat16), preferred_element_type=jnp.float32)
    rank1 = jnp.sum(jnp.where(lane == idx1, before, 0.0), axis=1, keepdims=True)
    rank2 = jnp.sum(jnp.where(lane == idx2, before, 0.0), axis=1, keepdims=True)
    eid_ref[...] = jnp.where(lane == 0, idx1, jnp.where(lane == 1, idx2, 0))
    rank_ref[...] = jnp.where(lane == 0, rank1, jnp.where(lane == 1, rank2, 0.0)).astype(jnp.int32)
    wts_ref[...] = jnp.where(lane == 0, w1, jnp.where(lane == 1, w2, 0.0))
    total = base_ref[...] + jnp.sum(hot.astype(jnp.float32), axis=0, keepdims=True)
    base_ref[...] = total
    cnt_ref[...] = total.astype(jnp.int32)


def _router(x, g, w_router):
    rows, d = x.shape
    bm = _tile(rows, ROUTER_BM)
    w_pad = jnp.pad(w_router.astype(jnp.float32), ((0, 0), (0, LANE - w_router.shape[1])))
    vmem = (2 * _nbytes((bm, d), jnp.float32) + 2 * _nbytes((d, LANE), jnp.float32)
            + 6 * _nbytes((bm, LANE), jnp.float32) + _nbytes((bm, bm), jnp.float32))
    tile = pl.BlockSpec((bm, LANE), lambda i: (i, 0))
    return pl.pallas_call(
        _router_body,
        grid=(rows // bm,),
        in_specs=[pl.BlockSpec((bm, d), lambda i: (i, 0)),
                  pl.BlockSpec((1, d), lambda i: (0, 0)),
                  pl.BlockSpec((d, LANE), lambda i: (0, 0))],
        out_specs=[tile, tile, tile, pl.BlockSpec((1, LANE), lambda i: (0, 0))],
        out_shape=[jax.ShapeDtypeStruct((rows, LANE), jnp.int32),
                   jax.ShapeDtypeStruct((rows, LANE), jnp.int32),
                   jax.ShapeDtypeStruct((rows, LANE), jnp.float32),
                   jax.ShapeDtypeStruct((1, LANE), jnp.int32)],
        scratch_shapes=[pltpu.VMEM((1, LANE), jnp.float32)],
        compiler_params=_params(("arbitrary",), vmem),
        name="router",
    )(x, g.reshape(1, d).astype(jnp.float32), w_pad)


def _moe_plan(eid, rank, counts, bm, n_tiles):
    cnt = counts[0, :N_EXPERTS]
    padded = (cnt + bm - 1) // bm * bm
    ends = jnp.cumsum(padded)
    starts = ends - padded
    e_sel = eid[:, :TOP_K]
    start_sel = jnp.zeros_like(e_sel)
    for e in range(N_EXPERTS):
        start_sel = jnp.where(e_sel == e, starts[e], start_sel)
    slot = (start_sel + rank[:, :TOP_K]).astype(jnp.int32)
    tokens = jnp.repeat(jnp.arange(slot.shape[0], dtype=jnp.int32), TOP_K)
    src = jnp.zeros((n_tiles * bm,), jnp.int32).at[slot.reshape(-1)].set(tokens, unique_indices=True)
    tile_start = jnp.arange(n_tiles, dtype=jnp.int32) * bm
    tile_expert = jnp.minimum(jnp.sum(tile_start[:, None] >= ends[None, :], axis=1), N_EXPERTS - 1)
    n_used = (ends[-1] // bm).reshape(1)
    return slot, src, tile_expert.astype(jnp.int32), n_used.astype(jnp.int32)


def _pack_bf16_pairs(y):
    half = y.shape[1] // 2
    bits = lax.bitcast_convert_type(y.astype(jnp.bfloat16).astype(jnp.float32), jnp.uint32)
    return bits[:, half:] | (bits[:, :half] >> 16)


def _unpack_bf16_pairs(words):
    lo = lax.bitcast_convert_type(words << 16, jnp.float32).astype(jnp.bfloat16)
    hi = lax.bitcast_convert_type(words & jnp.uint32(0xFFFF0000), jnp.float32).astype(jnp.bfloat16)
    return lo, hi


def _rmsnorm_packed_body(x_ref, g_ref, o_ref):
    x = x_ref[...]
    y = x * lax.rsqrt(jnp.mean(x * x, axis=-1, keepdims=True) + RMS_EPS) * g_ref[...]
    o_ref[...] = _pack_bf16_pairs(y)


def _rmsnorm_packed(x, g):
    rows, d = x.shape
    bm = _tile(rows, NORM_BM)
    vmem = 2 * _nbytes((bm, d), x.dtype) + 2 * _nbytes((bm, d // 2), jnp.uint32)
    return pl.pallas_call(
        _rmsnorm_packed_body,
        grid=(rows // bm,),
        in_specs=[pl.BlockSpec((bm, d), lambda i: (i, 0)), pl.BlockSpec((1, d), lambda i: (0, 0))],
        out_specs=pl.BlockSpec((bm, d // 2), lambda i: (i, 0)),
        out_shape=jax.ShapeDtypeStruct((rows, d // 2), jnp.uint32),
        compiler_params=_params(("parallel",), vmem),
        name="norm_ffn_packed",
    )(x, g.reshape(1, d).astype(jnp.float32))


def _gather_body(src_ref, src_next_ref, rows_hbm, o_ref, buf, sem):
    i = pl.program_id(0)
    n_steps = pl.num_programs(0)
    bt = o_ref.shape[0]

    def row_copy(src, half, r):
        return pltpu.make_async_copy(rows_hbm.at[pl.ds(src[0, r], 1)], buf.at[half, pl.ds(r, 1)],
                                     sem.at[half])

    def start_all(src, half):
        def body(r, _):
            row_copy(src, half, r).start()
            return 0
        lax.fori_loop(0, bt, body, 0, unroll=DMA_ISSUE_UNROLL)

    def wait_all(src, half):
        def body(r, _):
            row_copy(src, half, r).wait()
            return 0
        lax.fori_loop(0, bt, body, 0, unroll=DMA_ISSUE_UNROLL)

    cur = i % 2

    @pl.when(i == 0)
    def _():
        start_all(src_ref, 0)

    @pl.when(i + 1 < n_steps)
    def _():
        start_all(src_next_ref, 1 - cur)

    wait_all(src_ref, cur)
    o_ref[...] = buf[cur]


def _moe_gather(rows, src, bt):
    n_sorted = src.shape[0]
    width = rows.shape[1]
    n_steps = n_sorted // bt
    src_blk = src.reshape(n_steps, 1, bt)
    src_spec = lambda f: pl.BlockSpec((None, 1, bt), lambda i: (f(i), 0, 0), memory_space=pltpu.SMEM)
    vmem = 4 * _nbytes((bt, width), rows.dtype)
    return pl.pallas_call(
        _gather_body,
        grid=(n_steps,),
        in_specs=[src_spec(lambda i: i), src_spec(lambda i: jnp.minimum(i + 1, n_steps - 1)),
                  pl.BlockSpec(memory_space=pl.ANY)],
        out_specs=pl.BlockSpec((bt, width), lambda i: (i, 0)),
        out_shape=jax.ShapeDtypeStruct((n_sorted, width), rows.dtype),
        scratch_shapes=[pltpu.VMEM((2, bt, width), rows.dtype), pltpu.SemaphoreType.DMA((2,))],
        compiler_params=_params(("arbitrary",), vmem),
        name="moe_gather",
    )(src_blk, src_blk, rows)


def _grouped_body(packed, swiglu, te_ref, nu_ref, a_ref, *refs):
    del te_ref
    w_refs, o_ref = refs[:-1], refs[-1]
    i = pl.program_id(1)

    @pl.when(i < nu_ref[0])
    def _():
        if packed:
            lo, hi = _unpack_bf16_pairs(a_ref[...])
            half = lo.shape[1]
            prods = [jnp.dot(lo, w[pl.ds(0, half), :], preferred_element_type=jnp.float32)
                     + jnp.dot(hi, w[pl.ds(half, half), :], preferred_element_type=jnp.float32)
                     for w in w_refs]
        else:
            a = a_ref[...]
            prods = [jnp.dot(a, w[...], preferred_element_type=jnp.float32) for w in w_refs]
        o_ref[...] = (_ep_swiglu(prods) if swiglu else prods[0]).astype(o_ref.dtype)

    @pl.when(i >= nu_ref[0])
    def _():
        o_ref[...] = jnp.zeros_like(o_ref)


def _grouped_mm(name, a, weights, tile_expert, n_used, *, bm, packed, swiglu, out_dtype):
    m = a.shape[0]
    _, kdim, n = weights[0].shape
    bn = _tile(n, MM_BN_MULTI if len(weights) > 1 else MM_BN)
    w_spec = pl.BlockSpec((None, kdim, bn), lambda j, i, te, nu: (te[i], 0, j))
    vmem = (2 * _nbytes((bm, a.shape[1]), a.dtype) + 2 * len(weights) * _nbytes((kdim, bn), weights[0].dtype)
            + 2 * _nbytes((bm, bn), out_dtype) + (_nbytes((bm, kdim), jnp.bfloat16) if packed else 0))
    grid_spec = pltpu.PrefetchScalarGridSpec(
        num_scalar_prefetch=2,
        grid=(n // bn, m // bm),
        in_specs=[pl.BlockSpec((bm, a.shape[1]), lambda j, i, te, nu: (i, 0))] + [w_spec] * len(weights),
        out_specs=pl.BlockSpec((bm, bn), lambda j, i, te, nu: (i, j)),
    )
    return pl.pallas_call(
        functools.partial(_grouped_body, packed, swiglu),
        grid_spec=grid_spec,
        out_shape=jax.ShapeDtypeStruct((m, n), out_dtype),
        compiler_params=_params(("parallel", "arbitrary"), vmem),
        name=name,
    )(tile_expert, n_used, a, *weights)


def _combine_body(with_norm, slot_ref, slot_next_ref, x_ref, wts_ref, *refs):
    if with_norm:
        g_ref, y_hbm, o_ref, buf, sem = refs
    else:
        y_hbm, o_ref, buf, sem = refs
    i = pl.program_id(0)
    n_steps = pl.num_programs(0)
    bt = x_ref.shape[0]

    def row_copy(slots, half, r, c):
        return pltpu.make_async_copy(y_hbm.at[pl.ds(slots[0, r * TOP_K + c], 1)],
                                     buf.at[half, c, pl.ds(r, 1)], sem.at[half])

    def start_all(slots, half):
        def body(r, _):
            for c in range(TOP_K):
                row_copy(slots, half, r, c).start()
            return 0
        lax.fori_loop(0, bt, body, 0, unroll=DMA_ISSUE_UNROLL)

    def wait_all(slots, half):
        def body(r, _):
            for c in range(TOP_K):
                row_copy(slots, half, r, c).wait()
            return 0
        lax.fori_loop(0, bt, body, 0, unroll=DMA_ISSUE_UNROLL)

    cur = i % 2

    @pl.when(i == 0)
    def _():
        start_all(slot_ref, 0)

    @pl.when(i + 1 < n_steps)
    def _():
        start_all(slot_next_ref, 1 - cur)

    wait_all(slot_ref, cur)
    wts = wts_ref[...]
    lane = lax.broadcasted_iota(jnp.int32, wts.shape, 1)
    out = x_ref[...]
    for c in range(TOP_K):
        w_c = jnp.sum(jnp.where(lane == c, wts, 0.0), axis=1, keepdims=True)
        out = out + w_c * buf[cur, c]
    if with_norm:
        out = out * lax.rsqrt(jnp.mean(out * out, axis=-1, keepdims=True) + RMS_EPS) * g_ref[...]
    o_ref[...] = out.astype(o_ref.dtype)


def _moe_combine(x, y_sorted, slot, wts, g_final=None):
    t, d = x.shape
    bt = _tile(t, MOE_COMBINE_TILE)
    n_steps = t // bt
    slot_blk = slot.reshape(n_steps, 1, bt * TOP_K)
    slot_spec = lambda f: pl.BlockSpec((None, 1, bt * TOP_K), lambda i: (f(i), 0, 0), memory_space=pltpu.SMEM)
    in_specs = [slot_spec(lambda i: i), slot_spec(lambda i: jnp.minimum(i + 1, n_steps - 1)),
                pl.BlockSpec((bt, d), lambda i: (i, 0)), pl.BlockSpec((bt, LANE), lambda i: (i, 0))]
    args = [slot_blk, slot_blk, x, wts]
    if g_final is not None:
        in_specs.append(pl.BlockSpec((1, d), lambda i: (0, 0)))
        args.append(g_final.reshape(1, d).astype(jnp.float32))
    in_specs.append(pl.BlockSpec(memory_space=pl.ANY))
    args.append(y_sorted)
    vmem = (4 * _nbytes((bt, d), jnp.float32) + 2 * _nbytes((bt, LANE), jnp.float32)
            + 2 * TOP_K * _nbytes((bt, d), y_sorted.dtype))
    return pl.pallas_call(
        functools.partial(_combine_body, g_final is not None),
        grid=(n_steps,),
        in_specs=in_specs,
        out_specs=pl.BlockSpec((bt, d), lambda i: (i, 0)),
        out_shape=jax.ShapeDtypeStruct((t, d), x.dtype),
        scratch_shapes=[pltpu.VMEM((2, TOP_K, bt, d), y_sorted.dtype), pltpu.SemaphoreType.DMA((2,))],
        compiler_params=_params(("arbitrary",), vmem),
        name="moe_combine",
    )(*args)


def _moe_ffn(x2, g_ffn, w_router, w_gate, w_up, w_down, g_final=None):
    t, d = x2.shape
    bm = _tile(t, MOE_ROW_TILE)
    n_sorted = TOP_K * t + N_EXPERTS * bm
    eid, rank, wts, counts = _router(x2, g_ffn, w_router)
    slot, src, tile_expert, n_used = _moe_plan(eid, rank, counts, bm, n_sorted // bm)
    xn_sorted = _moe_gather(_rmsnorm_packed(x2, g_ffn), src, bm)
    h = _grouped_mm("moe_up", xn_sorted, [_bf16(w_gate), _bf16(w_up)], tile_expert, n_used,
                    bm=bm, packed=True, swiglu=True, out_dtype=jnp.bfloat16)
    y = _grouped_mm("moe_down", h, [_bf16(w_down)], tile_expert, n_used,
                    bm=bm, packed=False, swiglu=False, out_dtype=jnp.float32)
    return _moe_combine(x2, y, slot, wts, g_final)


def _swiglu_up(name, xn, w_gate, w_up):
    m, d = xn.shape
    f = w_gate.shape[1]
    bm, bn = _tile(m, MM_BM), _tile(f, MM_BN_MULTI)
    return _mm(name, [(xn, _a_spec(bm, d))], [(w_gate, _w_spec(d, bn)), (w_up, _w_spec(d, bn))], [],
               [(0, 0), (0, 1)], _ep_swiglu, m=m, n=f, bm=bm, bn=bn, nk=1, out_dtype=jnp.bfloat16)


def _bf16(w):
    return w.astype(jnp.bfloat16)


def _mixer(x2, batch, seq, g_mix, w_in, b_forget, b_gate, g_q, g_kv, w_uq, w_ukv, sinks, w_branch,
           w_out, tables):
    d = x2.shape[1]
    fox_w = FOX_HEADS * FOX_HEAD_DIM
    swa_w = SWA_Q_HEADS * SWA_HEAD_DIM
    swa_kv_w = SWA_KV_HEADS * SWA_HEAD_DIM
    mla_w = MLA_HEADS * MLA_V_DIM
    o_lat = 0
    o_fox = Q_LORA + KV_LORA + ROPE_DIM
    o_flog = o_fox + 3 * fox_w
    o_swa = o_flog + FOX_HEADS
    o_gate = o_swa + swa_w + 2 * swa_kv_w

    lat_w = Q_LORA + KV_LORA + LANE
    lat_pad = -(lat_w) % 512
    w_lat = _bf16(jnp.pad(w_in[:, o_lat:o_fox], ((0, 0), (0, LANE - ROPE_DIM + lat_pad))))
    w_fox = _bf16(w_in[:, o_fox:o_flog])
    w_flog = _bf16(jnp.pad(w_in[:, o_flog:o_swa], ((0, 0), (0, LANE - FOX_HEADS))))
    w_swa = _bf16(w_in[:, o_swa:o_gate])
    w_gate = _bf16(w_in[:, o_gate:])

    xn = _rmsnorm(x2, g_mix, jnp.bfloat16, name="norm_mix")
    lat = _linear("in_latent", xn, w_lat, jnp.float32, bn=512)
    fox_bn, swa_bn = _tile(3 * fox_w, MM_BN), _tile(swa_w + 2 * swa_kv_w, MM_BN)
    assert fox_w % fox_bn == 0 and swa_w % swa_bn == 0
    fqkv = _linear("in_fox", xn, w_fox, jnp.bfloat16, bn=fox_bn, epilogue=functools.partial(
        _ep_scale_leading, fox_w // fox_bn, FOX_HEAD_DIM ** -0.5 * LOG2E))
    flog = _linear("in_forget", xn, w_flog, jnp.float32)
    sqkv = _linear("in_swa", xn, w_swa, jnp.bfloat16, bn=swa_bn, epilogue=functools.partial(
        _ep_scale_leading, swa_w // swa_bn, SWA_HEAD_DIM ** -0.5 * LOG2E))
    gates = _linear("in_gate", xn, w_gate, jnp.bfloat16, epilogue=_ep_sigmoid_bias,
                    extras=[(b_gate.reshape(1, -1).astype(jnp.float32), "row")])

    cq_n = _rmsnorm(lat, g_q, jnp.bfloat16, col_block=0, width=Q_LORA, name="norm_q")
    ckv_n = _rmsnorm(lat, g_kv, jnp.bfloat16, col_block=Q_LORA // KV_LORA, width=KV_LORA, name="norm_kv")
    w_uq_p = w_uq.reshape(Q_LORA, MLA_HEADS, NOPE_DIM + ROPE_DIM)
    w_uq_p = _bf16(jnp.pad(w_uq_p, ((0, 0), (0, 0), (0, ROPE_DIM))).reshape(Q_LORA, MLA_HEADS * MLA_QK))
    w_ukv_h = w_ukv.reshape(KV_LORA, MLA_HEADS, NOPE_DIM + MLA_V_DIM)
    w_k = _bf16(w_ukv_h[:, :, :NOPE_DIM].reshape(KV_LORA, MLA_HEADS * NOPE_DIM))
    w_v = _bf16(w_ukv_h[:, :, NOPE_DIM:].reshape(KV_LORA, MLA_HEADS * MLA_V_DIM))
    q_mla = _mla_q(cq_n, w_uq_p, tables, seq)
    k_mla, v_mla = _mla_kv(ckv_n, w_k, w_v, lat, (Q_LORA + KV_LORA) // LANE, tables, seq)
    o_mla = _flash("attn_mla", q_mla, k_mla, v_mla, q_col0=0, k_col0=0, v_col0=0, dk=MLA_QK,
                   dv=MLA_V_DIM, heads=MLA_HEADS, batch=batch, seq=seq)

    b_pad = jnp.pad(b_forget.astype(jnp.float32), (0, LANE - FOX_HEADS)).reshape(1, LANE)
    cum = _forget_cumsum(flog, b_pad, batch, seq)
    bk = _tile(seq, FLASH_BK)
    cum_row = cum[:, :FOX_HEADS].reshape(batch, seq, FOX_HEADS).transpose(0, 2, 1)
    cum_row = cum_row.reshape(batch * FOX_HEADS, seq // bk, 1, bk)
    o_foxa = _flash("attn_fox", fqkv, fqkv, fqkv, q_col0=0, k_col0=FOX_HEADS, v_col0=2 * FOX_HEADS,
                    dk=FOX_HEAD_DIM, dv=FOX_HEAD_DIM, heads=FOX_HEADS, batch=batch, seq=seq,
                    cum_col=cum, cum_row=cum_row)

    o_swa_a = _swa(sqkv, sinks, batch, seq)

    m = x2.shape[0]
    bm, bn = _tile(m, MM_BM), _tile(d, MM_BN_MULTI)
    nj = d // bn
    wb = [_bf16(w_branch[:mla_w]), _bf16(w_branch[mla_w:mla_w + fox_w]), _bf16(w_branch[mla_w + fox_w:])]
    outs = [o_mla, o_foxa, o_swa_a]
    merged = _mm("branch_merge",
                 [(o, _a_spec(bm, o.shape[1])) for o in outs],
                 [(w, _w_spec(w.shape[0], bn)) for w in wb],
                 [(gates, _tile_spec(bm, bn, col_off=b * nj)) for b in range(N_BRANCHES)],
                 [(0, 0), (1, 1), (2, 2)], _ep_merge,
                 m=m, n=d, bm=bm, bn=bn, nk=1, out_dtype=jnp.bfloat16)
    return _linear("out_proj", merged, _bf16(w_out), jnp.float32, epilogue=_ep_residual,
                   extras=[(x2, "tile")])


def kernel(x, g_mix_norm, w_in, b_forget, b_gate, g_q_norm, g_kv_norm, w_uq, w_ukv, sinks, w_branch,
           w_out, g_ffn_norm, w_dense_gate, w_dense_up, w_dense_down, w_router, w_exp_gate, w_exp_up,
           w_exp_down, g_final):
    batch, seq, d = x.shape
    depth = w_in.shape[0]
    x2 = x.reshape(batch * seq, d)
    tables = _rope_tables(seq)
    for l in range(depth):
        x2 = _mixer(x2, batch, seq, g_mix_norm[l], w_in[l], b_forget[l], b_gate[l], g_q_norm[l],
                    g_kv_norm[l], w_uq[l], w_ukv[l], sinks[l], w_branch[l], w_out[l], tables)
        j = l // 2
        last = l == depth - 1
        if l % 2 == 0:
            xn = _rmsnorm(x2, g_ffn_norm[l], jnp.bfloat16, name="norm_ffn")
            h = _swiglu_up("dense_up", xn, _bf16(w_dense_gate[j]), _bf16(w_dense_up[j]))
            x2 = _linear("ffn_down", h, _bf16(w_dense_down[j]), jnp.float32, epilogue=_ep_residual,
                         extras=[(x2, "tile")])
            if last:
                x2 = _rmsnorm(x2, g_final, x.dtype, name="norm_final")
        else:
            x2 = _moe_ffn(x2, g_ffn_norm[l], w_router[j], w_exp_gate[j], w_exp_up[j], w_exp_down[j],
                          g_final if last else None)
    return x2.reshape(batch, seq, d)
```

```python
import functools

import jax
import jax.numpy as jnp
from jax import lax
from jax.experimental import pallas as pl
from jax.experimental.pallas import tpu as pltpu

RMS_EPS = 1e-6
MLA_HEADS = 16
Q_LORA = 1536
KV_LORA = 512
NOPE_DIM = 128
ROPE_DIM = 64
MLA_V_DIM = 128
ROPE_THETA = 10000.0
FOX_HEADS = 16
FOX_HEAD_DIM = 128
SWA_Q_HEADS = 32
SWA_KV_HEADS = 4
SWA_HEAD_DIM = 64
WINDOW = 128
N_BRANCHES = 3
N_EXPERTS = 8
TOP_K = 2

LANE = 128
V7X_VMEM_BYTES = 64 * 1024 * 1024
V7X_VMEM_BUDGET = 56 * 1024 * 1024
COMPILER_SCRATCH_BYTES = 12 * 1024 * 1024

MM_BM = 1024
MM_BN = 1024
MM_BK = 2048
MM_FULL_K = 4096
MM_BN_MULTI = 512
NORM_BM = 512
FLASH_BQ = 512
FLASH_BK = 1024
FLASH_HEADS_PER_STEP = 2
CUM_BLOCK = 512
MLA_KV_BM = 512
ROUTER_BM = 512
MOE_ROW_TILE = 512
MOE_COMBINE_TILE = 256
DMA_ISSUE_UNROLL = 8

MLA_QK = NOPE_DIM + 2 * ROPE_DIM
LOG2E = 1.4426950408889634


def _params(semantics, vmem_bytes):
    limit = min(V7X_VMEM_BUDGET, vmem_bytes + COMPILER_SCRATCH_BYTES)
    return pltpu.CompilerParams(dimension_semantics=semantics, vmem_limit_bytes=int(limit))


def _nbytes(shape, dtype):
    n = 1
    for s in shape:
        n *= s
    return n * jnp.dtype(dtype).itemsize


def _tile(full, want):
    t = min(full, want)
    while full % t:
        t //= 2
    return t


def _rmsnorm_body(x_ref, g_ref, o_ref):
    x = x_ref[...].astype(jnp.float32)
    y = x * lax.rsqrt(jnp.mean(x * x, axis=-1, keepdims=True) + RMS_EPS)
    o_ref[...] = (y * g_ref[...]).astype(o_ref.dtype)


def _rmsnorm(x, g, out_dtype, *, col_block=0, width=None, name="rmsnorm"):
    rows = x.shape[0]
    width = x.shape[1] if width is None else width
    bm = _tile(rows, NORM_BM)
    vmem = 2 * _nbytes((bm, width), x.dtype) + 2 * _nbytes((bm, width), out_dtype)
    return pl.pallas_call(
        _rmsnorm_body,
        grid=(rows // bm,),
        in_specs=[pl.BlockSpec((bm, width), lambda i: (i, col_block)),
                  pl.BlockSpec((1, width), lambda i: (0, 0))],
        out_specs=pl.BlockSpec((bm, width), lambda i: (i, 0)),
        out_shape=jax.ShapeDtypeStruct((rows, width), out_dtype),
        compiler_params=_params(("parallel",), vmem),
        name=name,
    )(x, g.reshape(1, width).astype(jnp.float32))


def _mm_body(n_a, n_w, n_x, dots, nk, epilogue, *refs):
    a = refs[:n_a]
    w = refs[n_a:n_a + n_w]
    xs = refs[n_a + n_w:n_a + n_w + n_x]
    o = refs[n_a + n_w + n_x]
    accs = refs[n_a + n_w + n_x + 1:]

    def products():
        return [jnp.dot(a[ai][...], w[wi][...], preferred_element_type=jnp.float32)
                for ai, wi in dots]

    def finish(vals):
        o[...] = epilogue(vals, *[x[...] for x in xs]).astype(o.dtype)

    if nk == 1:
        finish(products())
        return

    k = pl.program_id(2)

    @pl.when(k == 0)
    def _():
        for acc, p in zip(accs, products()):
            acc[...] = p

    @pl.when((k > 0) & (k < nk - 1))
    def _():
        for acc, p in zip(accs, products()):
            acc[...] += p

    @pl.when(k == nk - 1)
    def _():
        finish([acc[...] + p for acc, p in zip(accs, products())])


def _mm(name, a_ops, w_ops, x_ops, dots, epilogue, *, m, n, bm, bn, nk, out_dtype):
    ops = list(a_ops) + list(w_ops) + list(x_ops)
    vmem = 2 * _nbytes((bm, bn), out_dtype)
    for arr, spec in ops:
        blk = [d for d in spec.block_shape if d is not None]
        vmem += 2 * _nbytes(blk, arr.dtype)
    scratch = []
    if nk > 1:
        scratch = [pltpu.VMEM((bm, bn), jnp.float32) for _ in dots]
        vmem += len(dots) * _nbytes((bm, bn), jnp.float32)
    body = functools.partial(_mm_body, len(a_ops), len(w_ops), len(x_ops), tuple(dots), nk, epilogue)
    return pl.pallas_call(
        body,
        grid=(m // bm, n // bn, nk),
        in_specs=[spec for _, spec in ops],
        out_specs=pl.BlockSpec((bm, bn), lambda i, j, k: (i, j)),
        out_shape=jax.ShapeDtypeStruct((m, n), out_dtype),
        scratch_shapes=scratch,
        compiler_params=_params(("parallel", "parallel", "arbitrary"), vmem),
        name=name,
    )(*[arr for arr, _ in ops])


def _a_spec(bm, bk):
    return pl.BlockSpec((bm, bk), lambda i, j, k: (i, k))


def _w_spec(bk, bn):
    return pl.BlockSpec((bk, bn), lambda i, j, k: (k, j))


def _tile_spec(bm, bn, col_off=0):
    return pl.BlockSpec((bm, bn), lambda i, j, k: (i, j + col_off))


def _row_spec(bn, col_off=0):
    return pl.BlockSpec((1, bn), lambda i, j, k: (0, j + col_off))


def _ep_plain(vals):
    return vals[0]


def _ep_residual(vals, res):
    return res + vals[0]


def _ep_sigmoid_bias(vals, bias):
    return jax.nn.sigmoid(vals[0] + bias)


def _ep_scale_leading(n_blocks, factor, vals):
    return vals[0] * jnp.where(pl.program_id(1) < n_blocks, factor, 1.0)


def _ep_swiglu(vals):
    return jax.nn.silu(vals[0]) * vals[1]


def _ep_merge(vals, g0, g1, g2):
    return (g0.astype(jnp.float32) * vals[0] + g1.astype(jnp.float32) * vals[1]
            + g2.astype(jnp.float32) * vals[2])


def _linear(name, a, w, out_dtype, *, epilogue=_ep_plain, extras=(), bn=MM_BN):
    m, kdim = a.shape
    n = w.shape[1]
    bm, bn, bk = _tile(m, MM_BM), _tile(n, bn), _tile(kdim, MM_BK if kdim > MM_FULL_K else kdim)
    x_ops = []
    for arr, kind in extras:
        x_ops.append((arr, _tile_spec(bm, bn) if kind == "tile" else _row_spec(bn)))
    return _mm(name, [(a, _a_spec(bm, bk))], [(w, _w_spec(bk, bn))], x_ops, [(0, 0)], epilogue,
               m=m, n=n, bm=bm, bn=bn, nk=kdim // bk, out_dtype=out_dtype)


def _rope_tile(t, cos_t, sin_a, sin_b):
    half = ROPE_DIM // 2
    return (t * cos_t + pltpu.roll(t, LANE - half, axis=1) * sin_a
            + pltpu.roll(t, half, axis=1) * sin_b)


def _rope_tables(seq):
    half = ROPE_DIM // 2
    inv = ROPE_THETA ** (-jnp.arange(half, dtype=jnp.float32) / half)
    ang = jnp.arange(seq, dtype=jnp.float32)[:, None] * inv[None, :]
    cos, sin = jnp.cos(ang), jnp.sin(ang)
    z = jnp.zeros_like(cos)
    pad = jnp.zeros((seq, LANE - 2 * half), jnp.float32)
    cos_t = jnp.concatenate([cos, cos, pad], axis=1)
    sin_a = jnp.concatenate([-sin, z, pad], axis=1)
    sin_b = jnp.concatenate([z, sin, pad], axis=1)
    return cos_t, sin_a, sin_b


def _ep_mla_q(vals, cos_t, sin_a, sin_b):
    acc = vals[0] * ((NOPE_DIM + ROPE_DIM) ** -0.5 * LOG2E)
    pieces = []
    for h in range(acc.shape[1] // MLA_QK):
        lo = h * MLA_QK
        pieces.append(acc[:, lo:lo + NOPE_DIM])
        pieces.append(_rope_tile(acc[:, lo + NOPE_DIM:lo + MLA_QK], cos_t, sin_a, sin_b))
    return jnp.concatenate(pieces, axis=1)


def _mla_q(cq_n, w_uq_p, tables, seq):
    m, kdim = cq_n.shape
    n = w_uq_p.shape[1]
    bm = _tile(seq, MM_BM)
    bn = _tile(n, MM_BN)
    nseq = seq // bm
    tab_spec = pl.BlockSpec((bm, LANE), lambda i, j, k: (i % nseq, 0))
    return _mm("mla_q", [(cq_n, _a_spec(bm, kdim))], [(w_uq_p, _w_spec(kdim, bn))],
               [(t, tab_spec) for t in tables], [(0, 0)], _ep_mla_q,
               m=m, n=n, bm=bm, bn=bn, nk=1, out_dtype=jnp.bfloat16)


def _mla_kv_body(heads, a_ref, wk_ref, wv_ref, kr_ref, cos_ref, sina_ref, sinb_ref, k_ref, v_ref):
    a = a_ref[...]
    kn = jnp.dot(a, wk_ref[...], preferred_element_type=jnp.float32)
    v_ref[...] = jnp.dot(a, wv_ref[...], preferred_element_type=jnp.float32).astype(v_ref.dtype)
    kr = _rope_tile(kr_ref[...], cos_ref[...], sina_ref[...], sinb_ref[...]).astype(k_ref.dtype)
    for h in range(heads):
        k_ref[:, h * MLA_QK:h * MLA_QK + NOPE_DIM] = kn[:, h * NOPE_DIM:(h + 1) * NOPE_DIM].astype(k_ref.dtype)
        k_ref[:, h * MLA_QK + NOPE_DIM:(h + 1) * MLA_QK] = kr


def _mla_kv(ckv_n, w_k, w_v, lat, kr_col_block, tables, seq):
    m, kdim = ckv_n.shape
    heads = w_k.shape[1] // NOPE_DIM
    bm = _tile(seq, MLA_KV_BM)
    nseq = seq // bm
    tab_spec = pl.BlockSpec((bm, LANE), lambda i: (i % nseq, 0))
    nk_out, nv_out = heads * MLA_QK, heads * MLA_V_DIM
    vmem = (2 * _nbytes((bm, kdim), ckv_n.dtype) + 2 * _nbytes(w_k.shape, w_k.dtype)
            + 2 * _nbytes(w_v.shape, w_v.dtype) + 8 * _nbytes((bm, LANE), jnp.float32)
            + 2 * _nbytes((bm, nk_out), jnp.bfloat16) + 2 * _nbytes((bm, nv_out), jnp.bfloat16))
    return pl.pallas_call(
        functools.partial(_mla_kv_body, heads),
        grid=(m // bm,),
        in_specs=[pl.BlockSpec((bm, kdim), lambda i: (i, 0)),
                  pl.BlockSpec(w_k.shape, lambda i: (0, 0)),
                  pl.BlockSpec(w_v.shape, lambda i: (0, 0)),
                  pl.BlockSpec((bm, LANE), lambda i: (i, kr_col_block)),
                  tab_spec, tab_spec, tab_spec],
        out_specs=[pl.BlockSpec((bm, nk_out), lambda i: (i, 0)),
                   pl.BlockSpec((bm, nv_out), lambda i: (i, 0))],
        out_shape=[jax.ShapeDtypeStruct((m, nk_out), jnp.bfloat16),
                   jax.ShapeDtypeStruct((m, nv_out), jnp.bfloat16)],
        compiler_params=_params(("parallel",), vmem),
        name="mla_kv",
    )(ckv_n, w_k, w_v, lat, *tables)


def _split_bf16(x):
    hi = x.astype(jnp.bfloat16)
    r = x - hi.astype(jnp.float32)
    mid = r.astype(jnp.bfloat16)
    lo = (r - mid.astype(jnp.float32)).astype(jnp.bfloat16)
    return hi, mid, lo


def _cum_body(logit_ref, b_ref, o_ref, carry_ref):
    @pl.when(pl.program_id(1) == 0)
    def _():
        carry_ref[...] = jnp.zeros_like(carry_ref)

    z = logit_ref[...] + b_ref[...]
    log_f = (jnp.minimum(z, 0.0) - jnp.log1p(jnp.exp(-jnp.abs(z)))) * LOG2E
    n = log_f.shape[0]
    tri = (lax.broadcasted_iota(jnp.int32, (n, n), 0)
           >= lax.broadcasted_iota(jnp.int32, (n, n), 1)).astype(jnp.bfloat16)
    cum = carry_ref[...]
    for part in _split_bf16(log_f):
        cum = cum + jnp.dot(tri, part, preferred_element_type=jnp.float32)
    o_ref[...] = cum
    carry_ref[...] = cum[n - 1:n, :]


def _forget_cumsum(logit, b_pad, batch, seq):
    bm = _tile(seq, CUM_BLOCK)
    ns = seq // bm
    vmem = 4 * _nbytes((bm, LANE), jnp.float32) + _nbytes((bm, bm), jnp.float32)
    return pl.pallas_call(
        _cum_body,
        grid=(batch, ns),
        in_specs=[pl.BlockSpec((bm, LANE), lambda b, s: (b * ns + s, 0)),
                  pl.BlockSpec((1, LANE), lambda b, s: (0, 0))],
        out_specs=pl.BlockSpec((bm, LANE), lambda b, s: (b * ns + s, 0)),
        out_shape=jax.ShapeDtypeStruct(logit.shape, jnp.float32),
        scratch_shapes=[pltpu.VMEM((1, LANE), jnp.float32)],
        compiler_params=_params(("parallel", "arbitrary"), vmem),
        name="forget_cumsum",
    )(logit, b_pad)


def _flash_body(bq, bk, hp, dk, dv, with_cum, *refs):
    if with_cum:
        q_ref, k_ref, v_ref, cc_ref, cr_ref, o_ref = refs
    else:
        q_ref, k_ref, v_ref, o_ref = refs
    i = pl.program_id(2)
    wide = bk // bq
    qs = [q_ref[:, h * dk:(h + 1) * dk] for h in range(hp)]
    if with_cum:
        head0 = pl.program_id(1) * hp
        cc = cc_ref[...]
        lane = lax.broadcasted_iota(jnp.int32, cc.shape, 1)
        cum_q = [jnp.sum(jnp.where(lane == head0 + h, cc, 0.0), axis=1, keepdims=True)
                 for h in range(hp)]

    def step(j, carry, diag_offset=None):
        start = pl.multiple_of(j * bk, bk)
        diagonal = diag_offset is not None
        if diagonal:
            causal = (lax.broadcasted_iota(jnp.int32, (bq, bk), 0) + diag_offset
                      >= lax.broadcasted_iota(jnp.int32, (bq, bk), 1))
        out = []
        for h in range(hp):
            m, l, acc = carry[h]
            k = k_ref[pl.ds(start, bk), h * dk:(h + 1) * dk]
            v = v_ref[pl.ds(start, bk), h * dv:(h + 1) * dv]
            s = lax.dot_general(qs[h], k, (((1,), (1,)), ((), ())), preferred_element_type=jnp.float32)
            if with_cum:
                s = s - cr_ref[h, j]
            if diagonal:
                s = jnp.where(causal, s, -jnp.inf)
            row_max = jnp.max(s, axis=1, keepdims=True)
            if with_cum:
                m_new = jnp.maximum(m, row_max + cum_q[h])
                shift = m_new - cum_q[h]
            else:
                m_new = jnp.maximum(m, row_max)
                shift = m_new
            alpha = jnp.exp2(m - m_new)
            p = jnp.exp2(s - shift)
            l = alpha * l + jnp.sum(p, axis=1, keepdims=True)
            acc = alpha * acc + jnp.dot(p.astype(v.dtype), v, preferred_element_type=jnp.float32)
            out.append((m_new, l, acc))
        return tuple(out)

    init = tuple((jnp.full((bq, 1), -jnp.inf, jnp.float32), jnp.zeros((bq, 1), jnp.float32),
                  jnp.zeros((bq, dv), jnp.float32)) for _ in range(hp))
    n_wide = lax.div(i, wide)
    carry = lax.fori_loop(0, n_wide, step, init)
    final = step(n_wide, carry, lax.rem(i, wide) * bq)
    for h in range(hp):
        _, l, acc = final[h]
        o_ref[:, h * dv:(h + 1) * dv] = (acc / l).astype(o_ref.dtype)


def _flash(name, q_arr, k_arr, v_arr, *, q_col0, k_col0, v_col0, dk, dv, heads, batch, seq,
           cum_col=None, cum_row=None):
    bq = _tile(seq, FLASH_BQ)
    bk = _tile(seq, max(FLASH_BK, bq))
    assert bk % bq == 0
    nq, nk = seq // bq, seq // bk
    hp = FLASH_HEADS_PER_STEP
    assert heads % hp == 0 and q_col0 % hp == 0 and k_col0 % hp == 0 and v_col0 % hp == 0
    with_cum = cum_col is not None
    in_specs = [pl.BlockSpec((bq, hp * dk), lambda b, g, i: (b * nq + i, q_col0 // hp + g)),
                pl.BlockSpec((seq, hp * dk), lambda b, g, i: (b, k_col0 // hp + g)),
                pl.BlockSpec((seq, hp * dv), lambda b, g, i: (b, v_col0 // hp + g))]
    args = [q_arr, k_arr, v_arr]
    vmem = (2 * _nbytes((bq, hp * dk), q_arr.dtype) + 2 * _nbytes((seq, hp * dk), k_arr.dtype)
            + 2 * _nbytes((seq, hp * dv), v_arr.dtype) + 2 * _nbytes((bq, hp * dv), jnp.bfloat16))
    if with_cum:
        in_specs += [pl.BlockSpec((bq, LANE), lambda b, g, i: (b * nq + i, 0)),
                     pl.BlockSpec((hp, nk, 1, bk), lambda b, g, i: (b * (heads // hp) + g, 0, 0, 0))]
        args += [cum_col, cum_row.reshape(-1, nk, 1, bk)]
        vmem += 2 * _nbytes((bq, LANE), jnp.float32) + 2 * _nbytes((hp, nk, 8, bk), jnp.float32)
    return pl.pallas_call(
        functools.partial(_flash_body, bq, bk, hp, dk, dv, with_cum),
        grid=(batch, heads // hp, nq),
        in_specs=in_specs,
        out_specs=pl.BlockSpec((bq, hp * dv), lambda b, g, i: (b * nq + i, g)),
        out_shape=jax.ShapeDtypeStruct((batch * seq, heads * dv), jnp.bfloat16),
        compiler_params=_params(("parallel", "parallel", "arbitrary"), vmem),
        name=name,
    )(*args)


def _block_diag_pair(slab, head_in_slab):
    d = SWA_HEAD_DIM
    lane = lax.broadcasted_iota(jnp.int32, slab.shape, 1)
    x32 = slab.astype(jnp.float32)
    if head_in_slab == 0:
        left = jnp.where(lane < d, x32, 0.0)
        right = pltpu.roll(left, d, axis=1)
    else:
        right = jnp.where(lane >= d, x32, 0.0)
        left = pltpu.roll(right, d, axis=1)
    return jnp.concatenate([left, right], axis=0).astype(slab.dtype)


def _swa_body(sink_ref, q_ref, kp_ref, kc_ref, vp_ref, vc_ref, o_ref):
    n = pl.program_id(1)
    blk = q_ref.shape[0]
    group = SWA_Q_HEADS // SWA_KV_HEADS
    d = SWA_HEAD_DIM
    keys = 2 * blk
    row = lax.broadcasted_iota(jnp.int32, (blk, keys), 0)
    col = lax.broadcasted_iota(jnp.int32, (blk, keys), 1)
    dist = row + blk - col
    valid = (dist >= 0) & (dist < WINDOW) & ((col >= blk) | (n > 0))
    dist_f = dist.astype(jnp.float32)
    out_lane = lax.broadcasted_iota(jnp.int32, (blk, 2 * d), 1)
    for g in range(SWA_KV_HEADS):
        slab = slice((g // 2) * 2 * d, (g // 2 + 1) * 2 * d)
        k2 = _block_diag_pair(jnp.concatenate([kp_ref[:, slab], kc_ref[:, slab]], axis=0), g % 2)
        v2 = _block_diag_pair(jnp.concatenate([vp_ref[:, slab], vc_ref[:, slab]], axis=0), g % 2)
        for u in range(0, group, 2):
            h = g * group + u
            slopes = [LOG2E * 2.0 ** (-8.0 * (h + c + 1) / SWA_Q_HEADS) for c in range(2)]
            sinks = [sink_ref[h + c] * LOG2E for c in range(2)]
            q2 = q_ref[:, h * d:(h + 2) * d]
            s = lax.dot_general(q2, k2, (((1,), (1,)), ((), ())), preferred_element_type=jnp.float32)
            ps, denoms = [], []
            for c in range(2):
                s_c = jnp.where(valid, s[:, c * keys:(c + 1) * keys] - slopes[c] * dist_f, -jnp.inf)
                m = jnp.maximum(jnp.max(s_c, axis=1, keepdims=True), sinks[c])
                p = jnp.exp2(s_c - m)
                ps.append(p)
                denoms.append(jnp.sum(p, axis=1, keepdims=True) + jnp.exp2(sinks[c] - m))
            p2 = jnp.concatenate(ps, axis=1).astype(v2.dtype)
            o = jnp.dot(p2, v2, preferred_element_type=jnp.float32)
            o = o / jnp.where(out_lane < d, denoms[0], denoms[1])
            o_ref[:, h * d:(h + 2) * d] = o.astype(o_ref.dtype)


def _swa(sqkv, sinks, batch, seq):
    blk = WINDOW
    nb = seq // blk
    qw = SWA_Q_HEADS * SWA_HEAD_DIM
    kvw = SWA_KV_HEADS * SWA_HEAD_DIM
    k_col, v_col = qw // kvw, qw // kvw + 1
    cur = lambda col: pl.BlockSpec((blk, kvw), lambda b, n: (b * nb + n, col))
    prev = lambda col: pl.BlockSpec((blk, kvw), lambda b, n: (b * nb + jnp.maximum(n - 1, 0), col))
    vmem = 4 * _nbytes((blk, qw), jnp.bfloat16) + 8 * _nbytes((blk, kvw), jnp.bfloat16)
    return pl.pallas_call(
        _swa_body,
        grid=(batch, nb),
        in_specs=[pl.BlockSpec(memory_space=pltpu.SMEM),
                  pl.BlockSpec((blk, qw), lambda b, n: (b * nb + n, 0)),
                  prev(k_col), cur(k_col), prev(v_col), cur(v_col)],
        out_specs=pl.BlockSpec((blk, qw), lambda b, n: (b * nb + n, 0)),
        out_shape=jax.ShapeDtypeStruct((batch * seq, qw), jnp.bfloat16),
        compiler_params=_params(("parallel", "parallel"), vmem),
        name="swa",
    )(sinks.astype(jnp.float32), sqkv, sqkv, sqkv, sqkv, sqkv)


def _router_body(x_ref, g_ref, w_ref, eid_ref, rank_ref, wts_ref, cnt_ref, xn_ref, base_ref):
    @pl.when(pl.program_id(0) == 0)
    def _():
        base_ref[...] = jnp.zeros_like(base_ref)

    x = x_ref[...]
    xn = x * lax.rsqrt(jnp.mean(x * x, axis=-1, keepdims=True) + RMS_EPS) * g_ref[...]
    xn_ref[...] = _pack_bf16_pairs(xn)
    logits = jnp.dot(xn, w_ref[...], preferred_element_type=jnp.float32,
                     precision=lax.Precision.HIGHEST)
    lane = lax.broadcasted_iota(jnp.int32, logits.shape, 1)
    logits = jnp.where(lane < N_EXPERTS, logits, -jnp.inf)
    top1 = jnp.max(logits, axis=1, keepdims=True)
    idx1 = jnp.min(jnp.where(logits == top1, lane, LANE), axis=1, keepdims=True)
    rest = jnp.where(lane == idx1, -jnp.inf, logits)
    top2 = jnp.max(rest, axis=1, keepdims=True)
    idx2 = jnp.min(jnp.where(rest == top2, lane, LANE), axis=1, keepdims=True)
    e2 = jnp.exp(top2 - top1)
    w1 = 1.0 / (1.0 + e2)
    w2 = e2 / (1.0 + e2)
    n = x.shape[0]
    hot = (lane == idx1) | (lane == idx2)
    earlier = (lax.broadcasted_iota(jnp.int32, (n, n), 0)
               > lax.broadcasted_iota(jnp.int32, (n, n), 1)).astype(jnp.bfloat16)
    before = base_ref[...] + jnp.dot(earlier, hot.astype(jnp.bfloat16), preferred_element_type=jnp.float32)
    rank1 = jnp.sum(jnp.where(lane == idx1, before, 0.0), axis=1, keepdims=True)
    rank2 = jnp.sum(jnp.where(lane == idx2, before, 0.0), axis=1, keepdims=True)
    eid_ref[...] = jnp.where(lane == 0, idx1, jnp.where(lane == 1, idx2, 0))
    rank_ref[...] = jnp.where(lane == 0, rank1, jnp.where(lane == 1, rank2, 0.0)).astype(jnp.int32)
    wts_ref[...] = jnp.where(lane == 0, w1, jnp.where(lane == 1, w2, 0.0))
    total = base_ref[...] + jnp.sum(hot.astype(jnp.float32), axis=0, keepdims=True)
    base_ref[...] = total
    cnt_ref[...] = total.astype(jnp.int32)


def _router(x, g, w_router):
    rows, d = x.shape
    bm = _tile(rows, ROUTER_BM)
    w_pad = jnp.pad(w_router.astype(jnp.float32), ((0, 0), (0, LANE - w_router.shape[1])))
    vmem = (2 * _nbytes((bm, d), jnp.float32) + 2 * _nbytes((d, LANE), jnp.float32)
            + 6 * _nbytes((bm, LANE), jnp.float32) + _nbytes((bm, bm), jnp.float32)
            + 2 * _nbytes((bm, d // 2), jnp.uint32))
    tile = pl.BlockSpec((bm, LANE), lambda i: (i, 0))
    return pl.pallas_call(
        _router_body,
        grid=(rows // bm,),
        in_specs=[pl.BlockSpec((bm, d), lambda i: (i, 0)),
                  pl.BlockSpec((1, d), lambda i: (0, 0)),
                  pl.BlockSpec((d, LANE), lambda i: (0, 0))],
        out_specs=[tile, tile, tile, pl.BlockSpec((1, LANE), lambda i: (0, 0)),
                   pl.BlockSpec((bm, d // 2), lambda i: (i, 0))],
        out_shape=[jax.ShapeDtypeStruct((rows, LANE), jnp.int32),
                   jax.ShapeDtypeStruct((rows, LANE), jnp.int32),
                   jax.ShapeDtypeStruct((rows, LANE), jnp.float32),
                   jax.ShapeDtypeStruct((1, LANE), jnp.int32),
                   jax.ShapeDtypeStruct((rows, d // 2), jnp.uint32)],
        scratch_shapes=[pltpu.VMEM((1, LANE), jnp.float32)],
        compiler_params=_params(("arbitrary",), vmem),
        name="router",
    )(x, g.reshape(1, d).astype(jnp.float32), w_pad)


def _moe_plan(eid, rank, counts, bm, n_tiles):
    cnt = counts[0, :N_EXPERTS]
    padded = (cnt + bm - 1) // bm * bm
    ends = jnp.cumsum(padded)
    starts = ends - padded
    e_sel = eid[:, :TOP_K]
    start_sel = jnp.zeros_like(e_sel)
    for e in range(N_EXPERTS):
        start_sel = jnp.where(e_sel == e, starts[e], start_sel)
    slot = (start_sel + rank[:, :TOP_K]).astype(jnp.int32)
    tokens = jnp.repeat(jnp.arange(slot.shape[0], dtype=jnp.int32), TOP_K)
    src = jnp.zeros((n_tiles * bm,), jnp.int32).at[slot.reshape(-1)].set(tokens, unique_indices=True)
    tile_start = jnp.arange(n_tiles, dtype=jnp.int32) * bm
    tile_expert = jnp.minimum(jnp.sum(tile_start[:, None] >= ends[None, :], axis=1), N_EXPERTS - 1)
    n_used = (ends[-1] // bm).reshape(1)
    return slot, src, tile_expert.astype(jnp.int32), n_used.astype(jnp.int32)


def _pack_bf16_pairs(y):
    half = y.shape[1] // 2
    bits = lax.bitcast_convert_type(y.astype(jnp.bfloat16).astype(jnp.float32), jnp.uint32)
    return bits[:, half:] | (bits[:, :half] >> 16)


def _unpack_bf16_pairs(words):
    lo = lax.bitcast_convert_type(words << 16, jnp.float32).astype(jnp.bfloat16)
    hi = lax.bitcast_convert_type(words & jnp.uint32(0xFFFF0000), jnp.float32).astype(jnp.bfloat16)
    return lo, hi


def _gather_body(src_ref, src_next_ref, rows_hbm, o_ref, buf, sem):
    i = pl.program_id(0)
    n_steps = pl.num_programs(0)
    bt = o_ref.shape[0]

    def row_copy(src, half, r):
        return pltpu.make_async_copy(rows_hbm.at[pl.ds(src[0, r], 1)], buf.at[half, pl.ds(r, 1)],
                                     sem.at[half])

    def start_all(src, half):
        def body(r, _):
            row_copy(src, half, r).start()
            return 0
        lax.fori_loop(0, bt, body, 0, unroll=DMA_ISSUE_UNROLL)

    def wait_all(src, half):
        def body(r, _):
            row_copy(src, half, r).wait()
            return 0
        lax.fori_loop(0, bt, body, 0, unroll=DMA_ISSUE_UNROLL)

    cur = i % 2

    @pl.when(i == 0)
    def _():
        start_all(src_ref, 0)

    @pl.when(i + 1 < n_steps)
    def _():
        start_all(src_next_ref, 1 - cur)

    wait_all(src_ref, cur)
    o_ref[...] = buf[cur]


def _moe_gather(rows, src, bt):
    n_sorted = src.shape[0]
    width = rows.shape[1]
    n_steps = n_sorted // bt
    src_blk = src.reshape(n_steps, 1, bt)
    src_spec = lambda f: pl.BlockSpec((None, 1, bt), lambda i: (f(i), 0, 0), memory_space=pltpu.SMEM)
    vmem = 4 * _nbytes((bt, width), rows.dtype)
    return pl.pallas_call(
        _gather_body,
        grid=(n_steps,),
        in_specs=[src_spec(lambda i: i), src_spec(lambda i: jnp.minimum(i + 1, n_steps - 1)),
                  pl.BlockSpec(memory_space=pl.ANY)],
        out_specs=pl.BlockSpec((bt, width), lambda i: (i, 0)),
        out_shape=jax.ShapeDtypeStruct((n_sorted, width), rows.dtype),
        scratch_shapes=[pltpu.VMEM((2, bt, width), rows.dtype), pltpu.SemaphoreType.DMA((2,))],
        compiler_params=_params(("arbitrary",), vmem),
        name="moe_gather",
    )(src_blk, src_blk, rows)


def _grouped_body(packed, swiglu, te_ref, nu_ref, a_ref, *refs):
    del te_ref
    w_refs, o_ref = refs[:-1], refs[-1]
    i = pl.program_id(1)

    @pl.when(i < nu_ref[0])
    def _():
        if packed:
            lo, hi = _unpack_bf16_pairs(a_ref[...])
            half = lo.shape[1]
            prods = [jnp.dot(lo, w[pl.ds(0, half), :], preferred_element_type=jnp.float32)
                     + jnp.dot(hi, w[pl.ds(half, half), :], preferred_element_type=jnp.float32)
                     for w in w_refs]
        else:
            a = a_ref[...]
            prods = [jnp.dot(a, w[...], preferred_element_type=jnp.float32) for w in w_refs]
        o_ref[...] = (_ep_swiglu(prods) if swiglu else prods[0]).astype(o_ref.dtype)

    @pl.when(i >= nu_ref[0])
    def _():
        o_ref[...] = jnp.zeros_like(o_ref)


def _grouped_mm(name, a, weights, tile_expert, n_used, *, bm, packed, swiglu, out_dtype):
    m = a.shape[0]
    _, kdim, n = weights[0].shape
    bn = _tile(n, MM_BN_MULTI if len(weights) > 1 else MM_BN)
    w_spec = pl.BlockSpec((None, kdim, bn), lambda j, i, te, nu: (te[i], 0, j))
    vmem = (2 * _nbytes((bm, a.shape[1]), a.dtype) + 2 * len(weights) * _nbytes((kdim, bn), weights[0].dtype)
            + 2 * _nbytes((bm, bn), out_dtype) + (_nbytes((bm, kdim), jnp.bfloat16) if packed else 0))
    grid_spec = pltpu.PrefetchScalarGridSpec(
        num_scalar_prefetch=2,
        grid=(n // bn, m // bm),
        in_specs=[pl.BlockSpec((bm, a.shape[1]), lambda j, i, te, nu: (i, 0))] + [w_spec] * len(weights),
        out_specs=pl.BlockSpec((bm, bn), lambda j, i, te, nu: (i, j)),
    )
    return pl.pallas_call(
        functools.partial(_grouped_body, packed, swiglu),
        grid_spec=grid_spec,
        out_shape=jax.ShapeDtypeStruct((m, n), out_dtype),
        compiler_params=_params(("parallel", "arbitrary"), vmem),
        name=name,
    )(tile_expert, n_used, a, *weights)


def _combine_body(with_norm, slot_ref, slot_next_ref, x_ref, wts_ref, *refs):
    if with_norm:
        g_ref, y_hbm, o_ref, buf, sem = refs
    else:
        y_hbm, o_ref, buf, sem = refs
    i = pl.program_id(0)
    n_steps = pl.num_programs(0)
    bt = x_ref.shape[0]

    def row_copy(slots, half, r, c):
        return pltpu.make_async_copy(y_hbm.at[pl.ds(slots[0, r * TOP_K + c], 1)],
                                     buf.at[half, c, pl.ds(r, 1)], sem.at[half])

    def start_all(slots, half):
        def body(r, _):
            for c in range(TOP_K):
                row_copy(slots, half, r, c).start()
            return 0
        lax.fori_loop(0, bt, body, 0, unroll=DMA_ISSUE_UNROLL)

    def wait_all(slots, half):
        def body(r, _):
            for c in range(TOP_K):
                row_copy(slots, half, r, c).wait()
            return 0
        lax.fori_loop(0, bt, body, 0, unroll=DMA_ISSUE_UNROLL)

    cur = i % 2

    @pl.when(i == 0)
    def _():
        start_all(slot_ref, 0)

    @pl.when(i + 1 < n_steps)
    def _():
        start_all(slot_next_ref, 1 - cur)

    wait_all(slot_ref, cur)
    wts = wts_ref[...]
    lane = lax.broadcasted_iota(jnp.int32, wts.shape, 1)
    out = x_ref[...]
    for c in range(TOP_K):
        w_c = jnp.sum(jnp.where(lane == c, wts, 0.0), axis=1, keepdims=True)
        out = out + w_c * buf[cur, c]
    if with_norm:
        out = out * lax.rsqrt(jnp.mean(out * out, axis=-1, keepdims=True) + RMS_EPS) * g_ref[...]
    o_ref[...] = out.astype(o_ref.dtype)


def _moe_combine(x, y_sorted, slot, wts, g_final=None):
    t, d = x.shape
    bt = _tile(t, MOE_COMBINE_TILE)
    n_steps = t // bt
    slot_blk = slot.reshape(n_steps, 1, bt * TOP_K)
    slot_spec = lambda f: pl.BlockSpec((None, 1, bt * TOP_K), lambda i: (f(i), 0, 0), memory_space=pltpu.SMEM)
    in_specs = [slot_spec(lambda i: i), slot_spec(lambda i: jnp.minimum(i + 1, n_steps - 1)),
                pl.BlockSpec((bt, d), lambda i: (i, 0)), pl.BlockSpec((bt, LANE), lambda i: (i, 0))]
    args = [slot_blk, slot_blk, x, wts]
    if g_final is not None:
        in_specs.append(pl.BlockSpec((1, d), lambda i: (0, 0)))
        args.append(g_final.reshape(1, d).astype(jnp.float32))
    in_specs.append(pl.BlockSpec(memory_space=pl.ANY))
    args.append(y_sorted)
    vmem = (4 * _nbytes((bt, d), jnp.float32) + 2 * _nbytes((bt, LANE), jnp.float32)
            + 2 * TOP_K * _nbytes((bt, d), y_sorted.dtype))
    return pl.pallas_call(
        functools.partial(_combine_body, g_final is not None),
        grid=(n_steps,),
        in_specs=in_specs,
        out_specs=pl.BlockSpec((bt, d), lambda i: (i, 0)),
        out_shape=jax.ShapeDtypeStruct((t, d), x.dtype),
        scratch_shapes=[pltpu.VMEM((2, TOP_K, bt, d), y_sorted.dtype), pltpu.SemaphoreType.DMA((2,))],
        compiler_params=_params(("arbitrary",), vmem),
        name="moe_combine",
    )(*args)


def _moe_ffn(x2, g_ffn, w_router, w_gate, w_up, w_down, g_final=None):
    t, d = x2.shape
    bm = _tile(t, MOE_ROW_TILE)
    n_sorted = TOP_K * t + N_EXPERTS * bm
    eid, rank, wts, counts, xn_packed = _router(x2, g_ffn, w_router)
    slot, src, tile_expert, n_used = _moe_plan(eid, rank, counts, bm, n_sorted // bm)
    xn_sorted = _moe_gather(xn_packed, src, bm)
    h = _grouped_mm("moe_up", xn_sorted, [_bf16(w_gate), _bf16(w_up)], tile_expert, n_used,
                    bm=bm, packed=True, swiglu=True, out_dtype=jnp.bfloat16)
    y = _grouped_mm("moe_down", h, [_bf16(w_down)], tile_expert, n_used,
                    bm=bm, packed=False, swiglu=False, out_dtype=jnp.float32)
    return _moe_combine(x2, y, slot, wts, g_final)


def _swiglu_up(name, xn, w_gate, w_up):
    m, d = xn.shape
    f = w_gate.shape[1]
    bm, bn = _tile(m, MM_BM), _tile(f, MM_BN_MULTI)
    return _mm(name, [(xn, _a_spec(bm, d))], [(w_gate, _w_spec(d, bn)), (w_up, _w_spec(d, bn))], [],
               [(0, 0), (0, 1)], _ep_swiglu, m=m, n=f, bm=bm, bn=bn, nk=1, out_dtype=jnp.bfloat16)


def _bf16(w):
    return w.astype(jnp.bfloat16)


def _mixer(x2, batch, seq, g_mix, w_in, b_forget, b_gate, g_q, g_kv, w_uq, w_ukv, sinks, w_branch,
           w_out, tables):
    d = x2.shape[1]
    fox_w = FOX_HEADS * FOX_HEAD_DIM
    swa_w = SWA_Q_HEADS * SWA_HEAD_DIM
    swa_kv_w = SWA_KV_HEADS * SWA_HEAD_DIM
    mla_w = MLA_HEADS * MLA_V_DIM
    o_lat = 0
    o_fox = Q_LORA + KV_LORA + ROPE_DIM
    o_flog = o_fox + 3 * fox_w
    o_swa = o_flog + FOX_HEADS
    o_gate = o_swa + swa_w + 2 * swa_kv_w

    lat_w = Q_LORA + KV_LORA + LANE
    lat_pad = -(lat_w) % 512
    w_lat = _bf16(jnp.pad(w_in[:, o_lat:o_fox], ((0, 0), (0, LANE - ROPE_DIM + lat_pad))))
    w_fox = _bf16(w_in[:, o_fox:o_flog])
    w_flog = _bf16(jnp.pad(w_in[:, o_flog:o_swa], ((0, 0), (0, LANE - FOX_HEADS))))
    w_swa = _bf16(w_in[:, o_swa:o_gate])
    w_gate = _bf16(w_in[:, o_gate:])

    xn = _rmsnorm(x2, g_mix, jnp.bfloat16, name="norm_mix")
    lat = _linear("in_latent", xn, w_lat, jnp.float32, bn=512)
    fox_bn, swa_bn = _tile(3 * fox_w, MM_BN), _tile(swa_w + 2 * swa_kv_w, MM_BN)
    assert fox_w % fox_bn == 0 and swa_w % swa_bn == 0
    fqkv = _linear("in_fox", xn, w_fox, jnp.bfloat16, bn=fox_bn, epilogue=functools.partial(
        _ep_scale_leading, fox_w // fox_bn, FOX_HEAD_DIM ** -0.5 * LOG2E))
    flog = _linear("in_forget", xn, w_flog, jnp.float32)
    sqkv = _linear("in_swa", xn, w_swa, jnp.bfloat16, bn=swa_bn, epilogue=functools.partial(
        _ep_scale_leading, swa_w // swa_bn, SWA_HEAD_DIM ** -0.5 * LOG2E))
    gates = _linear("in_gate", xn, w_gate, jnp.bfloat16, epilogue=_ep_sigmoid_bias,
                    extras=[(b_gate.reshape(1, -1).astype(jnp.float32), "row")])

    cq_n = _rmsnorm(lat, g_q, jnp.bfloat16, col_block=0, width=Q_LORA, name="norm_q")
    ckv_n = _rmsnorm(lat, g_kv, jnp.bfloat16, col_block=Q_LORA // KV_LORA, width=KV_LORA, name="norm_kv")
    w_uq_p = w_uq.reshape(Q_LORA, MLA_HEADS, NOPE_DIM + ROPE_DIM)
    w_uq_p = _bf16(jnp.pad(w_uq_p, ((0, 0), (0, 0), (0, ROPE_DIM))).reshape(Q_LORA, MLA_HEADS * MLA_QK))
    w_ukv_h = w_ukv.reshape(KV_LORA, MLA_HEADS, NOPE_DIM + MLA_V_DIM)
    w_k = _bf16(w_ukv_h[:, :, :NOPE_DIM].reshape(KV_LORA, MLA_HEADS * NOPE_DIM))
    w_v = _bf16(w_ukv_h[:, :, NOPE_DIM:].reshape(KV_LORA, MLA_HEADS * MLA_V_DIM))
    q_mla = _mla_q(cq_n, w_uq_p, tables, seq)
    k_mla, v_mla = _mla_kv(ckv_n, w_k, w_v, lat, (Q_LORA + KV_LORA) // LANE, tables, seq)
    o_mla = _flash("attn_mla", q_mla, k_mla, v_mla, q_col0=0, k_col0=0, v_col0=0, dk=MLA_QK,
                   dv=MLA_V_DIM, heads=MLA_HEADS, batch=batch, seq=seq)

    b_pad = jnp.pad(b_forget.astype(jnp.float32), (0, LANE - FOX_HEADS)).reshape(1, LANE)
    cum = _forget_cumsum(flog, b_pad, batch, seq)
    cum_row = cum[:, :FOX_HEADS].reshape(batch, seq, FOX_HEADS).transpose(0, 2, 1)
    cum_row = cum_row.reshape(batch * FOX_HEADS, seq)
    o_foxa = _flash("attn_fox", fqkv, fqkv, fqkv, q_col0=0, k_col0=FOX_HEADS, v_col0=2 * FOX_HEADS,
                    dk=FOX_HEAD_DIM, dv=FOX_HEAD_DIM, heads=FOX_HEADS, batch=batch, seq=seq,
                    cum_col=cum, cum_row=cum_row)

    o_swa_a = _swa(sqkv, sinks, batch, seq)

    m = x2.shape[0]
    bm, bn = _tile(m, MM_BM), _tile(d, MM_BN_MULTI)
    nj = d // bn
    wb = [_bf16(w_branch[:mla_w]), _bf16(w_branch[mla_w:mla_w + fox_w]), _bf16(w_branch[mla_w + fox_w:])]
    outs = [o_mla, o_foxa, o_swa_a]
    merged = _mm("branch_merge",
                 [(o, _a_spec(bm, o.shape[1])) for o in outs],
                 [(w, _w_spec(w.shape[0], bn)) for w in wb],
                 [(gates, _tile_spec(bm, bn, col_off=b * nj)) for b in range(N_BRANCHES)],
                 [(0, 0), (1, 1), (2, 2)], _ep_merge,
                 m=m, n=d, bm=bm, bn=bn, nk=1, out_dtype=jnp.bfloat16)
    return _linear("out_proj", merged, _bf16(w_out), jnp.float32, epilogue=_ep_residual,
                   extras=[(x2, "tile")])


def kernel(x, g_mix_norm, w_in, b_forget, b_gate, g_q_norm, g_kv_norm, w_uq, w_ukv, sinks, w_branch,
           w_out, g_ffn_norm, w_dense_gate, w_dense_up, w_dense_down, w_router, w_exp_gate, w_exp_up,
           w_exp_down, g_final):
    batch, seq, d = x.shape
    depth = w_in.shape[0]
    x2 = x.reshape(batch * seq, d)
    tables = _rope_tables(seq)
    for l in range(depth):
        x2 = _mixer(x2, batch, seq, g_mix_norm[l], w_in[l], b_forget[l], b_gate[l], g_q_norm[l],
                    g_kv_norm[l], w_uq[l], w_ukv[l], sinks[l], w_branch[l], w_out[l], tables)
        j = l // 2
        last = l == depth - 1
        if l % 2 == 0:
            xn = _rmsnorm(x2, g_ffn_norm[l], jnp.bfloat16, name="norm_ffn")
            h = _swiglu_up("dense_up", xn, _bf16(w_dense_gate[j]), _bf16(w_dense_up[j]))
            x2 = _linear("ffn_down", h, _bf16(w_dense_down[j]), jnp.float32, epilogue=_ep_residual,
                         extras=[(x2, "tile")])
            if last:
                x2 = _rmsnorm(x2, g_final, x.dtype, name="norm_final")
        else:
            x2 = _moe_ffn(x2, g_ffn_norm[l], w_router[j], w_exp_gate[j], w_exp_up[j], w_exp_down[j],
                          g_final if last else None)
    return x2.reshape(batch, seq, d)
```

```python
import functools

import jax
import jax.numpy as jnp
from jax import lax
from jax.experimental import pallas as pl
from jax.experimental.pallas import tpu as pltpu

RMS_EPS = 1e-6
MLA_HEADS = 16
Q_LORA = 1536
KV_LORA = 512
NOPE_DIM = 128
ROPE_DIM = 64
MLA_V_DIM = 128
ROPE_THETA = 10000.0
FOX_HEADS = 16
FOX_HEAD_DIM = 128
SWA_Q_HEADS = 32
SWA_KV_HEADS = 4
SWA_HEAD_DIM = 64
WINDOW = 128
N_BRANCHES = 3
N_EXPERTS = 8
TOP_K = 2

LANE = 128
V7X_VMEM_BYTES = 64 * 1024 * 1024
V7X_VMEM_BUDGET = 56 * 1024 * 1024
COMPILER_SCRATCH_BYTES = 12 * 1024 * 1024

MM_BM = 1024
MM_BN = 1024
MM_BK = 2048
MM_FULL_K = 4096
MM_BN_MULTI = 512
NORM_BM = 512
FLASH_BQ = 512
FLASH_BK = 1024
FLASH_HEADS_PER_STEP = 2
CUM_BLOCK = 512
MLA_KV_BM = 512
ROUTER_BM = 512
MOE_ROW_TILE = 512
MOE_COMBINE_TILE = 256
DMA_ISSUE_UNROLL = 8

MLA_QK = NOPE_DIM + 2 * ROPE_DIM
LOG2E = 1.4426950408889634


def _params(semantics, vmem_bytes):
    limit = min(V7X_VMEM_BUDGET, vmem_bytes + COMPILER_SCRATCH_BYTES)
    return pltpu.CompilerParams(dimension_semantics=semantics, vmem_limit_bytes=int(limit))


def _nbytes(shape, dtype):
    n = 1
    for s in shape:
        n *= s
    return n * jnp.dtype(dtype).itemsize


def _tile(full, want):
    t = min(full, want)
    while full % t:
        t //= 2
    return t


def _rmsnorm_body(x_ref, g_ref, o_ref):
    x = x_ref[...].astype(jnp.float32)
    y = x * lax.rsqrt(jnp.mean(x * x, axis=-1, keepdims=True) + RMS_EPS)
    o_ref[...] = (y * g_ref[...]).astype(o_ref.dtype)


def _rmsnorm(x, g, out_dtype, *, col_block=0, width=None, name="rmsnorm"):
    rows = x.shape[0]
    width = x.shape[1] if width is None else width
    bm = _tile(rows, NORM_BM)
    vmem = 2 * _nbytes((bm, width), x.dtype) + 2 * _nbytes((bm, width), out_dtype)
    return pl.pallas_call(
        _rmsnorm_body,
        grid=(rows // bm,),
        in_specs=[pl.BlockSpec((bm, width), lambda i: (i, col_block)),
                  pl.BlockSpec((1, width), lambda i: (0, 0))],
        out_specs=pl.BlockSpec((bm, width), lambda i: (i, 0)),
        out_shape=jax.ShapeDtypeStruct((rows, width), out_dtype),
        compiler_params=_params(("parallel",), vmem),
        name=name,
    )(x, g.reshape(1, width).astype(jnp.float32))


def _mm_body(n_a, n_w, n_x, dots, nk, epilogue, *refs):
    a = refs[:n_a]
    w = refs[n_a:n_a + n_w]
    xs = refs[n_a + n_w:n_a + n_w + n_x]
    o = refs[n_a + n_w + n_x]
    accs = refs[n_a + n_w + n_x + 1:]

    def products():
        return [jnp.dot(a[ai][...], w[wi][...], preferred_element_type=jnp.float32)
                for ai, wi in dots]

    def finish(vals):
        o[...] = epilogue(vals, *[x[...] for x in xs]).astype(o.dtype)

    if nk == 1:
        finish(products())
        return

    k = pl.program_id(2)

    @pl.when(k == 0)
    def _():
        for acc, p in zip(accs, products()):
            acc[...] = p

    @pl.when((k > 0) & (k < nk - 1))
    def _():
        for acc, p in zip(accs, products()):
            acc[...] += p

    @pl.when(k == nk - 1)
    def _():
        finish([acc[...] + p for acc, p in zip(accs, products())])


def _mm(name, a_ops, w_ops, x_ops, dots, epilogue, *, m, n, bm, bn, nk, out_dtype):
    ops = list(a_ops) + list(w_ops) + list(x_ops)
    vmem = 2 * _nbytes((bm, bn), out_dtype)
    for arr, spec in ops:
        blk = [d for d in spec.block_shape if d is not None]
        vmem += 2 * _nbytes(blk, arr.dtype)
    scratch = []
    if nk > 1:
        scratch = [pltpu.VMEM((bm, bn), jnp.float32) for _ in dots]
        vmem += len(dots) * _nbytes((bm, bn), jnp.float32)
    body = functools.partial(_mm_body, len(a_ops), len(w_ops), len(x_ops), tuple(dots), nk, epilogue)
    return pl.pallas_call(
        body,
        grid=(m // bm, n // bn, nk),
        in_specs=[spec for _, spec in ops],
        out_specs=pl.BlockSpec((bm, bn), lambda i, j, k: (i, j)),
        out_shape=jax.ShapeDtypeStruct((m, n), out_dtype),
        scratch_shapes=scratch,
        compiler_params=_params(("parallel", "parallel", "arbitrary"), vmem),
        name=name,
    )(*[arr for arr, _ in ops])


def _a_spec(bm, bk):
    return pl.BlockSpec((bm, bk), lambda i, j, k: (i, k))


def _w_spec(bk, bn):
    return pl.BlockSpec((bk, bn), lambda i, j, k: (k, j))


def _tile_spec(bm, bn, col_off=0):
    return pl.BlockSpec((bm, bn), lambda i, j, k: (i, j + col_off))


def _row_spec(bn, col_off=0):
    return pl.BlockSpec((1, bn), lambda i, j, k: (0, j + col_off))


def _ep_plain(vals):
    return vals[0]


def _ep_residual(vals, res):
    return res + vals[0]


def _ep_sigmoid_bias(vals, bias):
    return jax.nn.sigmoid(vals[0] + bias)


def _ep_scale_leading(n_blocks, factor, vals):
    return vals[0] * jnp.where(pl.program_id(1) < n_blocks, factor, 1.0)


def _ep_swiglu(vals):
    return jax.nn.silu(vals[0]) * vals[1]


def _ep_merge(vals, g0, g1, g2):
    return (g0.astype(jnp.float32) * vals[0] + g1.astype(jnp.float32) * vals[1]
            + g2.astype(jnp.float32) * vals[2])


def _linear(name, a, w, out_dtype, *, epilogue=_ep_plain, extras=(), bn=MM_BN):
    m, kdim = a.shape
    n = w.shape[1]
    bm, bn, bk = _tile(m, MM_BM), _tile(n, bn), _tile(kdim, MM_BK if kdim > MM_FULL_K else kdim)
    x_ops = []
    for arr, kind in extras:
        x_ops.append((arr, _tile_spec(bm, bn) if kind == "tile" else _row_spec(bn)))
    return _mm(name, [(a, _a_spec(bm, bk))], [(w, _w_spec(bk, bn))], x_ops, [(0, 0)], epilogue,
               m=m, n=n, bm=bm, bn=bn, nk=kdim // bk, out_dtype=out_dtype)


def _rope_tile(t, cos_t, sin_a, sin_b):
    half = ROPE_DIM // 2
    return (t * cos_t + pltpu.roll(t, LANE - half, axis=1) * sin_a
            + pltpu.roll(t, half, axis=1) * sin_b)


def _rope_tables(seq):
    half = ROPE_DIM // 2
    inv = ROPE_THETA ** (-jnp.arange(half, dtype=jnp.float32) / half)
    ang = jnp.arange(seq, dtype=jnp.float32)[:, None] * inv[None, :]
    cos, sin = jnp.cos(ang), jnp.sin(ang)
    z = jnp.zeros_like(cos)
    pad = jnp.zeros((seq, LANE - 2 * half), jnp.float32)
    cos_t = jnp.concatenate([cos, cos, pad], axis=1)
    sin_a = jnp.concatenate([-sin, z, pad], axis=1)
    sin_b = jnp.concatenate([z, sin, pad], axis=1)
    return cos_t, sin_a, sin_b


def _ep_mla_q(vals, cos_t, sin_a, sin_b):
    acc = vals[0] * ((NOPE_DIM + ROPE_DIM) ** -0.5 * LOG2E)
    pieces = []
    for h in range(acc.shape[1] // MLA_QK):
        lo = h * MLA_QK
        pieces.append(acc[:, lo:lo + NOPE_DIM])
        pieces.append(_rope_tile(acc[:, lo + NOPE_DIM:lo + MLA_QK], cos_t, sin_a, sin_b))
    return jnp.concatenate(pieces, axis=1)


def _mla_q(cq_n, w_uq_p, tables, seq):
    m, kdim = cq_n.shape
    n = w_uq_p.shape[1]
    bm = _tile(seq, MM_BM)
    bn = _tile(n, MM_BN)
    nseq = seq // bm
    tab_spec = pl.BlockSpec((bm, LANE), lambda i, j, k: (i % nseq, 0))
    return _mm("mla_q", [(cq_n, _a_spec(bm, kdim))], [(w_uq_p, _w_spec(kdim, bn))],
               [(t, tab_spec) for t in tables], [(0, 0)], _ep_mla_q,
               m=m, n=n, bm=bm, bn=bn, nk=1, out_dtype=jnp.bfloat16)


def _mla_kv_body(heads, c_ref, g_ref, wk_ref, wv_ref, kr_ref, cos_ref, sina_ref, sinb_ref, k_ref, v_ref):
    c = c_ref[...]
    a = (c * lax.rsqrt(jnp.mean(c * c, axis=-1, keepdims=True) + RMS_EPS) * g_ref[...]).astype(wk_ref.dtype)
    kn = jnp.dot(a, wk_ref[...], preferred_element_type=jnp.float32)
    v_ref[...] = jnp.dot(a, wv_ref[...], preferred_element_type=jnp.float32).astype(v_ref.dtype)
    kr = _rope_tile(kr_ref[...], cos_ref[...], sina_ref[...], sinb_ref[...]).astype(k_ref.dtype)
    for h in range(heads):
        k_ref[:, h * MLA_QK:h * MLA_QK + NOPE_DIM] = kn[:, h * NOPE_DIM:(h + 1) * NOPE_DIM].astype(k_ref.dtype)
        k_ref[:, h * MLA_QK + NOPE_DIM:(h + 1) * MLA_QK] = kr


def _mla_kv(lat, ckv_col_block, g_kv, w_k, w_v, kr_col_block, tables, seq):
    m = lat.shape[0]
    kdim = w_k.shape[0]
    heads = w_k.shape[1] // NOPE_DIM
    bm = _tile(seq, MLA_KV_BM)
    nseq = seq // bm
    tab_spec = pl.BlockSpec((bm, LANE), lambda i: (i % nseq, 0))
    nk_out, nv_out = heads * MLA_QK, heads * MLA_V_DIM
    vmem = (2 * _nbytes((bm, kdim), lat.dtype) + 2 * _nbytes(w_k.shape, w_k.dtype)
            + 2 * _nbytes(w_v.shape, w_v.dtype) + 8 * _nbytes((bm, LANE), jnp.float32)
            + 2 * _nbytes((bm, nk_out), jnp.bfloat16) + 2 * _nbytes((bm, nv_out), jnp.bfloat16))
    return pl.pallas_call(
        functools.partial(_mla_kv_body, heads),
        grid=(m // bm,),
        in_specs=[pl.BlockSpec((bm, kdim), lambda i: (i, ckv_col_block)),
                  pl.BlockSpec((1, kdim), lambda i: (0, 0)),
                  pl.BlockSpec(w_k.shape, lambda i: (0, 0)),
                  pl.BlockSpec(w_v.shape, lambda i: (0, 0)),
                  pl.BlockSpec((bm, LANE), lambda i: (i, kr_col_block)),
                  tab_spec, tab_spec, tab_spec],
        out_specs=[pl.BlockSpec((bm, nk_out), lambda i: (i, 0)),
                   pl.BlockSpec((bm, nv_out), lambda i: (i, 0))],
        out_shape=[jax.ShapeDtypeStruct((m, nk_out), jnp.bfloat16),
                   jax.ShapeDtypeStruct((m, nv_out), jnp.bfloat16)],
        compiler_params=_params(("parallel",), vmem),
        name="mla_kv",
    )(lat, g_kv.reshape(1, kdim).astype(jnp.float32), w_k, w_v, lat, *tables)


def _split_bf16(x):
    hi = x.astype(jnp.bfloat16)
    r = x - hi.astype(jnp.float32)
    mid = r.astype(jnp.bfloat16)
    lo = (r - mid.astype(jnp.float32)).astype(jnp.bfloat16)
    return hi, mid, lo


def _cum_body(logit_ref, b_ref, o_ref, carry_ref):
    @pl.when(pl.program_id(1) == 0)
    def _():
        carry_ref[...] = jnp.zeros_like(carry_ref)

    z = logit_ref[...] + b_ref[...]
    log_f = (jnp.minimum(z, 0.0) - jnp.log1p(jnp.exp(-jnp.abs(z)))) * LOG2E
    n = log_f.shape[0]
    tri = (lax.broadcasted_iota(jnp.int32, (n, n), 0)
           >= lax.broadcasted_iota(jnp.int32, (n, n), 1)).astype(jnp.bfloat16)
    cum = carry_ref[...]
    for part in _split_bf16(log_f):
        cum = cum + jnp.dot(tri, part, preferred_element_type=jnp.float32)
    o_ref[...] = cum
    carry_ref[...] = cum[n - 1:n, :]


def _forget_cumsum(logit, col_block, b_pad, batch, seq):
    bm = _tile(seq, CUM_BLOCK)
    ns = seq // bm
    vmem = 4 * _nbytes((bm, LANE), jnp.float32) + _nbytes((bm, bm), jnp.float32)
    return pl.pallas_call(
        _cum_body,
        grid=(batch, ns),
        in_specs=[pl.BlockSpec((bm, LANE), lambda b, s: (b * ns + s, col_block)),
                  pl.BlockSpec((1, LANE), lambda b, s: (0, 0))],
        out_specs=pl.BlockSpec((bm, LANE), lambda b, s: (b * ns + s, 0)),
        out_shape=jax.ShapeDtypeStruct((logit.shape[0], LANE), jnp.float32),
        scratch_shapes=[pltpu.VMEM((1, LANE), jnp.float32)],
        compiler_params=_params(("parallel", "arbitrary"), vmem),
        name="forget_cumsum",
    )(logit, b_pad)


def _flash_body(bq, bk, hp, dk, dv, with_cum, *refs):
    if with_cum:
        q_ref, k_ref, v_ref, cc_ref, cr_ref, o_ref = refs
    else:
        q_ref, k_ref, v_ref, o_ref = refs
    i = pl.program_id(2)
    wide = bk // bq
    qs = [q_ref[:, h * dk:(h + 1) * dk] for h in range(hp)]
    if with_cum:
        head0 = pl.program_id(1) * hp
        cc = cc_ref[...]
        lane = lax.broadcasted_iota(jnp.int32, cc.shape, 1)
        cum_q = [jnp.sum(jnp.where(lane == head0 + h, cc, 0.0), axis=1, keepdims=True)
                 for h in range(hp)]

    def step(j, carry, diag_offset=None):
        start = pl.multiple_of(j * bk, bk)
        diagonal = diag_offset is not None
        if diagonal:
            causal = (lax.broadcasted_iota(jnp.int32, (bq, bk), 0) + diag_offset
                      >= lax.broadcasted_iota(jnp.int32, (bq, bk), 1))
        out = []
        for h in range(hp):
            m, l, acc = carry[h]
            k = k_ref[pl.ds(start, bk), h * dk:(h + 1) * dk]
            v = v_ref[pl.ds(start, bk), h * dv:(h + 1) * dv]
            s = lax.dot_general(qs[h], k, (((1,), (1,)), ((), ())), preferred_element_type=jnp.float32)
            if with_cum:
                s = s - cr_ref[h, j]
            if diagonal:
                s = jnp.where(causal, s, -jnp.inf)
            row_max = jnp.max(s, axis=1, keepdims=True)
            if with_cum:
                m_new = jnp.maximum(m, row_max + cum_q[h])
                shift = m_new - cum_q[h]
            else:
                m_new = jnp.maximum(m, row_max)
                shift = m_new
            alpha = jnp.exp2(m - m_new)
            p = jnp.exp2(s - shift)
            l = alpha * l + jnp.sum(p, axis=1, keepdims=True)
            acc = alpha * acc + jnp.dot(p.astype(v.dtype), v, preferred_element_type=jnp.float32)
            out.append((m_new, l, acc))
        return tuple(out)

    init = tuple((jnp.full((bq, 1), -jnp.inf, jnp.float32), jnp.zeros((bq, 1), jnp.float32),
                  jnp.zeros((bq, dv), jnp.float32)) for _ in range(hp))
    n_wide = lax.div(i, wide)
    carry = lax.fori_loop(0, n_wide, step, init)
    final = step(n_wide, carry, lax.rem(i, wide) * bq)
    for h in range(hp):
        _, l, acc = final[h]
        o_ref[:, h * dv:(h + 1) * dv] = (acc / l).astype(o_ref.dtype)


def _flash(name, q_arr, k_arr, v_arr, *, q_col0, k_col0, v_col0, dk, dv, heads, batch, seq,
           cum_col=None, cum_row=None):
    bq = _tile(seq, FLASH_BQ)
    bk = _tile(seq, max(FLASH_BK, bq))
    assert bk % bq == 0
    nq, nk = seq // bq, seq // bk
    hp = FLASH_HEADS_PER_STEP
    assert heads % hp == 0 and q_col0 % hp == 0 and k_col0 % hp == 0 and v_col0 % hp == 0
    with_cum = cum_col is not None
    in_specs = [pl.BlockSpec((bq, hp * dk), lambda b, g, i: (b * nq + i, q_col0 // hp + g)),
                pl.BlockSpec((seq, hp * dk), lambda b, g, i: (b, k_col0 // hp + g)),
                pl.BlockSpec((seq, hp * dv), lambda b, g, i: (b, v_col0 // hp + g))]
    args = [q_arr, k_arr, v_arr]
    vmem = (2 * _nbytes((bq, hp * dk), q_arr.dtype) + 2 * _nbytes((seq, hp * dk), k_arr.dtype)
            + 2 * _nbytes((seq, hp * dv), v_arr.dtype) + 2 * _nbytes((bq, hp * dv), jnp.bfloat16))
    if with_cum:
        in_specs += [pl.BlockSpec((bq, LANE), lambda b, g, i: (b * nq + i, 0)),
                     pl.BlockSpec((hp, nk, 1, bk), lambda b, g, i: (b * (heads // hp) + g, 0, 0, 0))]
        args += [cum_col, cum_row.reshape(-1, nk, 1, bk)]
        vmem += 2 * _nbytes((bq, LANE), jnp.float32) + 2 * _nbytes((hp, nk, 8, bk), jnp.float32)
    return pl.pallas_call(
        functools.partial(_flash_body, bq, bk, hp, dk, dv, with_cum),
        grid=(batch, heads // hp, nq),
        in_specs=in_specs,
        out_specs=pl.BlockSpec((bq, hp * dv), lambda b, g, i: (b * nq + i, g)),
        out_shape=jax.ShapeDtypeStruct((batch * seq, heads * dv), jnp.bfloat16),
        compiler_params=_params(("parallel", "parallel", "arbitrary"), vmem),
        name=name,
    )(*args)


def _block_diag_pair(slab, head_in_slab):
    d = SWA_HEAD_DIM
    lane = lax.broadcasted_iota(jnp.int32, slab.shape, 1)
    x32 = slab.astype(jnp.float32)
    if head_in_slab == 0:
        left = jnp.where(lane < d, x32, 0.0)
        right = pltpu.roll(left, d, axis=1)
    else:
        right = jnp.where(lane >= d, x32, 0.0)
        left = pltpu.roll(right, d, axis=1)
    return jnp.concatenate([left, right], axis=0).astype(slab.dtype)


def _swa_body(sink_ref, q_ref, kp_ref, kc_ref, vp_ref, vc_ref, o_ref):
    n = pl.program_id(1)
    blk = q_ref.shape[0]
    group = SWA_Q_HEADS // SWA_KV_HEADS
    d = SWA_HEAD_DIM
    keys = 2 * blk
    row = lax.broadcasted_iota(jnp.int32, (blk, keys), 0)
    col = lax.broadcasted_iota(jnp.int32, (blk, keys), 1)
    dist = row + blk - col
    valid = (dist >= 0) & (dist < WINDOW) & ((col >= blk) | (n > 0))
    dist_f = dist.astype(jnp.float32)
    out_lane = lax.broadcasted_iota(jnp.int32, (blk, 2 * d), 1)
    for g in range(SWA_KV_HEADS):
        slab = slice((g // 2) * 2 * d, (g // 2 + 1) * 2 * d)
        k2 = _block_diag_pair(jnp.concatenate([kp_ref[:, slab], kc_ref[:, slab]], axis=0), g % 2)
        v2 = _block_diag_pair(jnp.concatenate([vp_ref[:, slab], vc_ref[:, slab]], axis=0), g % 2)
        for u in range(0, group, 2):
            h = g * group + u
            slopes = [LOG2E * 2.0 ** (-8.0 * (h + c + 1) / SWA_Q_HEADS) for c in range(2)]
            sinks = [sink_ref[h + c] * LOG2E for c in range(2)]
            q2 = q_ref[:, h * d:(h + 2) * d]
            s = lax.dot_general(q2, k2, (((1,), (1,)), ((), ())), preferred_element_type=jnp.float32)
            ps, denoms = [], []
            for c in range(2):
                s_c = jnp.where(valid, s[:, c * keys:(c + 1) * keys] - slopes[c] * dist_f, -jnp.inf)
                m = jnp.maximum(jnp.max(s_c, axis=1, keepdims=True), sinks[c])
                p = jnp.exp2(s_c - m)
                ps.append(p)
                denoms.append(jnp.sum(p, axis=1, keepdims=True) + jnp.exp2(sinks[c] - m))
            p2 = jnp.concatenate(ps, axis=1).astype(v2.dtype)
            o = jnp.dot(p2, v2, preferred_element_type=jnp.float32)
            o = o / jnp.where(out_lane < d, denoms[0], denoms[1])
            o_ref[:, h * d:(h + 2) * d] = o.astype(o_ref.dtype)


def _swa(sqkv, sinks, batch, seq):
    blk = WINDOW
    nb = seq // blk
    qw = SWA_Q_HEADS * SWA_HEAD_DIM
    kvw = SWA_KV_HEADS * SWA_HEAD_DIM
    k_col, v_col = qw // kvw, qw // kvw + 1
    cur = lambda col: pl.BlockSpec((blk, kvw), lambda b, n: (b * nb + n, col))
    prev = lambda col: pl.BlockSpec((blk, kvw), lambda b, n: (b * nb + jnp.maximum(n - 1, 0), col))
    vmem = 4 * _nbytes((blk, qw), jnp.bfloat16) + 8 * _nbytes((blk, kvw), jnp.bfloat16)
    return pl.pallas_call(
        _swa_body,
        grid=(batch, nb),
        in_specs=[pl.BlockSpec(memory_space=pltpu.SMEM),
                  pl.BlockSpec((blk, qw), lambda b, n: (b * nb + n, 0)),
                  prev(k_col), cur(k_col), prev(v_col), cur(v_col)],
        out_specs=pl.BlockSpec((blk, qw), lambda b, n: (b * nb + n, 0)),
        out_shape=jax.ShapeDtypeStruct((batch * seq, qw), jnp.bfloat16),
        compiler_params=_params(("parallel", "parallel"), vmem),
        name="swa",
    )(sinks.astype(jnp.float32), sqkv, sqkv, sqkv, sqkv, sqkv)


def _router_body(x_ref, g_ref, w_ref, eid_ref, rank_ref, wts_ref, cnt_ref, xn_ref, base_ref):
    @pl.when(pl.program_id(0) == 0)
    def _():
        base_ref[...] = jnp.zeros_like(base_ref)

    x = x_ref[...]
    xn = x * lax.rsqrt(jnp.mean(x * x, axis=-1, keepdims=True) + RMS_EPS) * g_ref[...]
    xn_ref[...] = _pack_bf16_pairs(xn)
    logits = jnp.dot(xn, w_ref[...], preferred_element_type=jnp.float32,
                     precision=lax.Precision.HIGHEST)
    lane = lax.broadcasted_iota(jnp.int32, logits.shape, 1)
    logits = jnp.where(lane < N_EXPERTS, logits, -jnp.inf)
    top1 = jnp.max(logits, axis=1, keepdims=True)
    idx1 = jnp.min(jnp.where(logits == top1, lane, LANE), axis=1, keepdims=True)
    rest = jnp.where(lane == idx1, -jnp.inf, logits)
    top2 = jnp.max(rest, axis=1, keepdims=True)
    idx2 = jnp.min(jnp.where(rest == top2, lane, LANE), axis=1, keepdims=True)
    e2 = jnp.exp(top2 - top1)
    w1 = 1.0 / (1.0 + e2)
    w2 = e2 / (1.0 + e2)
    n = x.shape[0]
    hot = (lane == idx1) | (lane == idx2)
    earlier = (lax.broadcasted_iota(jnp.int32, (n, n), 0)
               > lax.broadcasted_iota(jnp.int32, (n, n), 1)).astype(jnp.bfloat16)
    before = base_ref[...] + jnp.dot(earlier, hot.astype(jnp.bfloat16), preferred_element_type=jnp.float32)
    rank1 = jnp.sum(jnp.where(lane == idx1, before, 0.0), axis=1, keepdims=True)
    rank2 = jnp.sum(jnp.where(lane == idx2, before, 0.0), axis=1, keepdims=True)
    eid_ref[...] = jnp.where(lane == 0, idx1, jnp.where(lane == 1, idx2, 0))
    rank_ref[...] = jnp.where(lane == 0, rank1, jnp.where(lane == 1, rank2, 0.0)).astype(jnp.int32)
    wts_ref[...] = jnp.where(lane == 0, w1, jnp.where(lane == 1, w2, 0.0))
    total = base_ref[...] + jnp.sum(hot.astype(jnp.float32), axis=0, keepdims=True)
    base_ref[...] = total
    cnt_ref[...] = total.astype(jnp.int32)


def _router(x, g, w_router):
    rows, d = x.shape
    bm = _tile(rows, ROUTER_BM)
    w_pad = jnp.pad(w_router.astype(jnp.float32), ((0, 0), (0, LANE - w_router.shape[1])))
    vmem = (2 * _nbytes((bm, d), jnp.float32) + 2 * _nbytes((d, LANE), jnp.float32)
            + 6 * _nbytes((bm, LANE), jnp.float32) + _nbytes((bm, bm), jnp.float32)
            + 2 * _nbytes((bm, d // 2), jnp.uint32))
    tile = pl.BlockSpec((bm, LANE), lambda i: (i, 0))
    return pl.pallas_call(
        _router_body,
        grid=(rows // bm,),
        in_specs=[pl.BlockSpec((bm, d), lambda i: (i, 0)),
                  pl.BlockSpec((1, d), lambda i: (0, 0)),
                  pl.BlockSpec((d, LANE), lambda i: (0, 0))],
        out_specs=[tile, tile, tile, pl.BlockSpec((1, LANE), lambda i: (0, 0)),
                   pl.BlockSpec((bm, d // 2), lambda i: (i, 0))],
        out_shape=[jax.ShapeDtypeStruct((rows, LANE), jnp.int32),
                   jax.ShapeDtypeStruct((rows, LANE), jnp.int32),
                   jax.ShapeDtypeStruct((rows, LANE), jnp.float32),
                   jax.ShapeDtypeStruct((1, LANE), jnp.int32),
                   jax.ShapeDtypeStruct((rows, d // 2), jnp.uint32)],
        scratch_shapes=[pltpu.VMEM((1, LANE), jnp.float32)],
        compiler_params=_params(("arbitrary",), vmem),
        name="router",
    )(x, g.reshape(1, d).astype(jnp.float32), w_pad)


def _moe_plan(eid, rank, counts, bm, n_tiles):
    cnt = counts[0, :N_EXPERTS]
    padded = (cnt + bm - 1) // bm * bm
    ends = jnp.cumsum(padded)
    starts = ends - padded
    e_sel = eid[:, :TOP_K]
    start_sel = jnp.zeros_like(e_sel)
    for e in range(N_EXPERTS):
        start_sel = jnp.where(e_sel == e, starts[e], start_sel)
    slot = (start_sel + rank[:, :TOP_K]).astype(jnp.int32)
    tokens = jnp.repeat(jnp.arange(slot.shape[0], dtype=jnp.int32), TOP_K)
    src = jnp.zeros((n_tiles * bm,), jnp.int32).at[slot.reshape(-1)].set(tokens, unique_indices=True)
    tile_start = jnp.arange(n_tiles, dtype=jnp.int32) * bm
    tile_expert = jnp.minimum(jnp.sum(tile_start[:, None] >= ends[None, :], axis=1), N_EXPERTS - 1)
    n_used = (ends[-1] // bm).reshape(1)
    return slot, src, tile_expert.astype(jnp.int32), n_used.astype(jnp.int32)


def _pack_bf16_pairs(y):
    half = y.shape[1] // 2
    bits = lax.bitcast_convert_type(y.astype(jnp.bfloat16).astype(jnp.float32), jnp.uint32)
    return bits[:, half:] | (bits[:, :half] >> 16)


def _unpack_bf16_pairs(words):
    lo = lax.bitcast_convert_type(words << 16, jnp.float32).astype(jnp.bfloat16)
    hi = lax.bitcast_convert_type(words & jnp.uint32(0xFFFF0000), jnp.float32).astype(jnp.bfloat16)
    return lo, hi


def _gather_body(src_ref, src_next_ref, rows_hbm, o_ref, buf, sem):
    i = pl.program_id(0)
    n_steps = pl.num_programs(0)
    bt = o_ref.shape[0]

    def row_copy(src, half, r):
        return pltpu.make_async_copy(rows_hbm.at[pl.ds(src[0, r], 1)], buf.at[half, pl.ds(r, 1)],
                                     sem.at[half])

    def start_all(src, half):
        def body(r, _):
            row_copy(src, half, r).start()
            return 0
        lax.fori_loop(0, bt, body, 0, unroll=DMA_ISSUE_UNROLL)

    def wait_all(src, half):
        def body(r, _):
            row_copy(src, half, r).wait()
            return 0
        lax.fori_loop(0, bt, body, 0, unroll=DMA_ISSUE_UNROLL)

    cur = i % 2

    @pl.when(i == 0)
    def _():
        start_all(src_ref, 0)

    @pl.when(i + 1 < n_steps)
    def _():
        start_all(src_next_ref, 1 - cur)

    wait_all(src_ref, cur)
    o_ref[...] = buf[cur]


def _moe_gather(rows, src, bt):
    n_sorted = src.shape[0]
    width = rows.shape[1]
    n_steps = n_sorted // bt
    src_blk = src.reshape(n_steps, 1, bt)
    src_spec = lambda f: pl.BlockSpec((None, 1, bt), lambda i: (f(i), 0, 0), memory_space=pltpu.SMEM)
    vmem = 4 * _nbytes((bt, width), rows.dtype)
    return pl.pallas_call(
        _gather_body,
        grid=(n_steps,),
        in_specs=[src_spec(lambda i: i), src_spec(lambda i: jnp.minimum(i + 1, n_steps - 1)),
                  pl.BlockSpec(memory_space=pl.ANY)],
        out_specs=pl.BlockSpec((bt, width), lambda i: (i, 0)),
        out_shape=jax.ShapeDtypeStruct((n_sorted, width), rows.dtype),
        scratch_shapes=[pltpu.VMEM((2, bt, width), rows.dtype), pltpu.SemaphoreType.DMA((2,))],
        compiler_params=_params(("arbitrary",), vmem),
        name="moe_gather",
    )(src_blk, src_blk, rows)


def _grouped_body(packed, swiglu, te_ref, nu_ref, a_ref, *refs):
    del te_ref
    w_refs, o_ref = refs[:-1], refs[-1]
    i = pl.program_id(1)

    @pl.when(i < nu_ref[0])
    def _():
        if packed:
            lo, hi = _unpack_bf16_pairs(a_ref[...])
            half = lo.shape[1]
            prods = [jnp.dot(lo, w[pl.ds(0, half), :], preferred_element_type=jnp.float32)
                     + jnp.dot(hi, w[pl.ds(half, half), :], preferred_element_type=jnp.float32)
                     for w in w_refs]
        else:
            a = a_ref[...]
            prods = [jnp.dot(a, w[...], preferred_element_type=jnp.float32) for w in w_refs]
        o_ref[...] = (_ep_swiglu(prods) if swiglu else prods[0]).astype(o_ref.dtype)

    @pl.when(i >= nu_ref[0])
    def _():
        o_ref[...] = jnp.zeros_like(o_ref)


def _grouped_mm(name, a, weights, tile_expert, n_used, *, bm, packed, swiglu, out_dtype):
    m = a.shape[0]
    _, kdim, n = weights[0].shape
    bn = _tile(n, MM_BN_MULTI if len(weights) > 1 else MM_BN)
    w_spec = pl.BlockSpec((None, kdim, bn), lambda j, i, te, nu: (te[i], 0, j))
    vmem = (2 * _nbytes((bm, a.shape[1]), a.dtype) + 2 * len(weights) * _nbytes((kdim, bn), weights[0].dtype)
            + 2 * _nbytes((bm, bn), out_dtype) + (_nbytes((bm, kdim), jnp.bfloat16) if packed else 0))
    grid_spec = pltpu.PrefetchScalarGridSpec(
        num_scalar_prefetch=2,
        grid=(n // bn, m // bm),
        in_specs=[pl.BlockSpec((bm, a.shape[1]), lambda j, i, te, nu: (i, 0))] + [w_spec] * len(weights),
        out_specs=pl.BlockSpec((bm, bn), lambda j, i, te, nu: (i, j)),
    )
    return pl.pallas_call(
        functools.partial(_grouped_body, packed, swiglu),
        grid_spec=grid_spec,
        out_shape=jax.ShapeDtypeStruct((m, n), out_dtype),
        compiler_params=_params(("parallel", "arbitrary"), vmem),
        name=name,
    )(tile_expert, n_used, a, *weights)


def _combine_body(with_norm, slot_ref, slot_next_ref, x_ref, wts_ref, *refs):
    if with_norm:
        g_ref, y_hbm, o_ref, buf, sem = refs
    else:
        y_hbm, o_ref, buf, sem = refs
    i = pl.program_id(0)
    n_steps = pl.num_programs(0)
    bt = x_ref.shape[0]

    def row_copy(slots, half, r, c):
        return pltpu.make_async_copy(y_hbm.at[pl.ds(slots[0, r * TOP_K + c], 1)],
                                     buf.at[half, c, pl.ds(r, 1)], sem.at[half])

    def start_all(slots, half):
        def body(r, _):
            for c in range(TOP_K):
                row_copy(slots, half, r, c).start()
            return 0
        lax.fori_loop(0, bt, body, 0, unroll=DMA_ISSUE_UNROLL)

    def wait_all(slots, half):
        def body(r, _):
            for c in range(TOP_K):
                row_copy(slots, half, r, c).wait()
            return 0
        lax.fori_loop(0, bt, body, 0, unroll=DMA_ISSUE_UNROLL)

    cur = i % 2

    @pl.when(i == 0)
    def _():
        start_all(slot_ref, 0)

    @pl.when(i + 1 < n_steps)
    def _():
        start_all(slot_next_ref, 1 - cur)

    wait_all(slot_ref, cur)
    wts = wts_ref[...]
    lane = lax.broadcasted_iota(jnp.int32, wts.shape, 1)
    out = x_ref[...]
    for c in range(TOP_K):
        w_c = jnp.sum(jnp.where(lane == c, wts, 0.0), axis=1, keepdims=True)
        out = out + w_c * buf[cur, c]
    if with_norm:
        out = out * lax.rsqrt(jnp.mean(out * out, axis=-1, keepdims=True) + RMS_EPS) * g_ref[...]
    o_ref[...] = out.astype(o_ref.dtype)


def _moe_combine(x, y_sorted, slot, wts, g_final=None):
    t, d = x.shape
    bt = _tile(t, MOE_COMBINE_TILE)
    n_steps = t // bt
    slot_blk = slot.reshape(n_steps, 1, bt * TOP_K)
    slot_spec = lambda f: pl.BlockSpec((None, 1, bt * TOP_K), lambda i: (f(i), 0, 0), memory_space=pltpu.SMEM)
    in_specs = [slot_spec(lambda i: i), slot_spec(lambda i: jnp.minimum(i + 1, n_steps - 1)),
                pl.BlockSpec((bt, d), lambda i: (i, 0)), pl.BlockSpec((bt, LANE), lambda i: (i, 0))]
    args = [slot_blk, slot_blk, x, wts]
    if g_final is not None:
        in_specs.append(pl.BlockSpec((1, d), lambda i: (0, 0)))
        args.append(g_final.reshape(1, d).astype(jnp.float32))
    in_specs.append(pl.BlockSpec(memory_space=pl.ANY))
    args.append(y_sorted)
    vmem = (4 * _nbytes((bt, d), jnp.float32) + 2 * _nbytes((bt, LANE), jnp.float32)
            + 2 * TOP_K * _nbytes((bt, d), y_sorted.dtype))
    return pl.pallas_call(
        functools.partial(_combine_body, g_final is not None),
        grid=(n_steps,),
        in_specs=in_specs,
        out_specs=pl.BlockSpec((bt, d), lambda i: (i, 0)),
        out_shape=jax.ShapeDtypeStruct((t, d), x.dtype),
        scratch_shapes=[pltpu.VMEM((2, TOP_K, bt, d), y_sorted.dtype), pltpu.SemaphoreType.DMA((2,))],
        compiler_params=_params(("arbitrary",), vmem),
        name="moe_combine",
    )(*args)


def _moe_ffn(x2, g_ffn, w_router, w_gate, w_up, w_down, g_final=None):
    t, d = x2.shape
    bm = _tile(t, MOE_ROW_TILE)
    n_sorted = TOP_K * t + N_EXPERTS * bm
    eid, rank, wts, counts, xn_packed = _router(x2, g_ffn, w_router)
    slot, src, tile_expert, n_used = _moe_plan(eid, rank, counts, bm, n_sorted // bm)
    xn_sorted = _moe_gather(xn_packed, src, bm)
    h = _grouped_mm("moe_up", xn_sorted, [_bf16(w_gate), _bf16(w_up)], tile_expert, n_used,
                    bm=bm, packed=True, swiglu=True, out_dtype=jnp.bfloat16)
    y = _grouped_mm("moe_down", h, [_bf16(w_down)], tile_expert, n_used,
                    bm=bm, packed=False, swiglu=False, out_dtype=jnp.float32)
    return _moe_combine(x2, y, slot, wts, g_final)


def _swiglu_up(name, xn, w_gate, w_up):
    m, d = xn.shape
    f = w_gate.shape[1]
    bm, bn = _tile(m, MM_BM), _tile(f, MM_BN_MULTI)
    return _mm(name, [(xn, _a_spec(bm, d))], [(w_gate, _w_spec(d, bn)), (w_up, _w_spec(d, bn))], [],
               [(0, 0), (0, 1)], _ep_swiglu, m=m, n=f, bm=bm, bn=bn, nk=1, out_dtype=jnp.bfloat16)


def _bf16(w):
    return w.astype(jnp.bfloat16)


def _mixer(x2, batch, seq, g_mix, w_in, b_forget, b_gate, g_q, g_kv, w_uq, w_ukv, sinks, w_branch,
           w_out, tables):
    d = x2.shape[1]
    fox_w = FOX_HEADS * FOX_HEAD_DIM
    swa_w = SWA_Q_HEADS * SWA_HEAD_DIM
    swa_kv_w = SWA_KV_HEADS * SWA_HEAD_DIM
    mla_w = MLA_HEADS * MLA_V_DIM
    o_lat = 0
    o_fox = Q_LORA + KV_LORA + ROPE_DIM
    o_flog = o_fox + 3 * fox_w
    o_swa = o_flog + FOX_HEADS
    o_gate = o_swa + swa_w + 2 * swa_kv_w

    lat_w = Q_LORA + KV_LORA + 2 * LANE
    lat_pad = -(lat_w) % 512
    w_lat = _bf16(jnp.concatenate([
        jnp.pad(w_in[:, o_lat:o_fox], ((0, 0), (0, LANE - ROPE_DIM))),
        jnp.pad(w_in[:, o_flog:o_swa], ((0, 0), (0, LANE - FOX_HEADS + lat_pad)))], axis=1))
    flog_col_block = (Q_LORA + KV_LORA + LANE) // LANE
    w_fox = _bf16(w_in[:, o_fox:o_flog])
    w_swa = _bf16(w_in[:, o_swa:o_gate])
    w_gate = _bf16(w_in[:, o_gate:])

    xn = _rmsnorm(x2, g_mix, jnp.bfloat16, name="norm_mix")
    lat = _linear("in_latent", xn, w_lat, jnp.float32, bn=512)
    fox_bn, swa_bn = _tile(3 * fox_w, MM_BN), _tile(swa_w + 2 * swa_kv_w, MM_BN)
    assert fox_w % fox_bn == 0 and swa_w % swa_bn == 0
    fqkv = _linear("in_fox", xn, w_fox, jnp.bfloat16, bn=fox_bn, epilogue=functools.partial(
        _ep_scale_leading, fox_w // fox_bn, FOX_HEAD_DIM ** -0.5 * LOG2E))
    sqkv = _linear("in_swa", xn, w_swa, jnp.bfloat16, bn=swa_bn, epilogue=functools.partial(
        _ep_scale_leading, swa_w // swa_bn, SWA_HEAD_DIM ** -0.5 * LOG2E))
    gates = _linear("in_gate", xn, w_gate, jnp.bfloat16, epilogue=_ep_sigmoid_bias,
                    extras=[(b_gate.reshape(1, -1).astype(jnp.float32), "row")])

    cq_n = _rmsnorm(lat, g_q, jnp.bfloat16, col_block=0, width=Q_LORA, name="norm_q")
    w_uq_p = w_uq.reshape(Q_LORA, MLA_HEADS, NOPE_DIM + ROPE_DIM)
    w_uq_p = _bf16(jnp.pad(w_uq_p, ((0, 0), (0, 0), (0, ROPE_DIM))).reshape(Q_LORA, MLA_HEADS * MLA_QK))
    w_ukv_h = w_ukv.reshape(KV_LORA, MLA_HEADS, NOPE_DIM + MLA_V_DIM)
    w_k = _bf16(w_ukv_h[:, :, :NOPE_DIM].reshape(KV_LORA, MLA_HEADS * NOPE_DIM))
    w_v = _bf16(w_ukv_h[:, :, NOPE_DIM:].reshape(KV_LORA, MLA_HEADS * MLA_V_DIM))
    q_mla = _mla_q(cq_n, w_uq_p, tables, seq)
    k_mla, v_mla = _mla_kv(lat, Q_LORA // KV_LORA, g_kv, w_k, w_v, (Q_LORA + KV_LORA) // LANE, tables, seq)
    o_mla = _flash("attn_mla", q_mla, k_mla, v_mla, q_col0=0, k_col0=0, v_col0=0, dk=MLA_QK,
                   dv=MLA_V_DIM, heads=MLA_HEADS, batch=batch, seq=seq)

    b_pad = jnp.pad(b_forget.astype(jnp.float32), (0, LANE - FOX_HEADS)).reshape(1, LANE)
    cum = _forget_cumsum(lat, flog_col_block, b_pad, batch, seq)
    cum_row = cum[:, :FOX_HEADS].reshape(batch, seq, FOX_HEADS).transpose(0, 2, 1)
    cum_row = cum_row.reshape(batch * FOX_HEADS, seq)
    o_foxa = _flash("attn_fox", fqkv, fqkv, fqkv, q_col0=0, k_col0=FOX_HEADS, v_col0=2 * FOX_HEADS,
                    dk=FOX_HEAD_DIM, dv=FOX_HEAD_DIM, heads=FOX_HEADS, batch=batch, seq=seq,
                    cum_col=cum, cum_row=cum_row)

    o_swa_a = _swa(sqkv, sinks, batch, seq)

    m = x2.shape[0]
    bm, bn = _tile(m, MM_BM), _tile(d, MM_BN_MULTI)
    nj = d // bn
    wb = [_bf16(w_branch[:mla_w]), _bf16(w_branch[mla_w:mla_w + fox_w]), _bf16(w_branch[mla_w + fox_w:])]
    outs = [o_mla, o_foxa, o_swa_a]
    merged = _mm("branch_merge",
                 [(o, _a_spec(bm, o.shape[1])) for o in outs],
                 [(w, _w_spec(w.shape[0], bn)) for w in wb],
                 [(gates, _tile_spec(bm, bn, col_off=b * nj)) for b in range(N_BRANCHES)],
                 [(0, 0), (1, 1), (2, 2)], _ep_merge,
                 m=m, n=d, bm=bm, bn=bn, nk=1, out_dtype=jnp.bfloat16)
    return _linear("out_proj", merged, _bf16(w_out), jnp.float32, epilogue=_ep_residual,
                   extras=[(x2, "tile")])


def kernel(x, g_mix_norm, w_in, b_forget, b_gate, g_q_norm, g_kv_norm, w_uq, w_ukv, sinks, w_branch,
           w_out, g_ffn_norm, w_dense_gate, w_dense_up, w_dense_down, w_router, w_exp_gate, w_exp_up,
           w_exp_down, g_final):
    batch, seq, d = x.shape
    depth = w_in.shape[0]
    x2 = x.reshape(batch * seq, d)
    tables = _rope_tables(seq)
    for l in range(depth):
        x2 = _mixer(x2, batch, seq, g_mix_norm[l], w_in[l], b_forget[l], b_gate[l], g_q_norm[l],
                    g_kv_norm[l], w_uq[l], w_ukv[l], sinks[l], w_branch[l], w_out[l], tables)
        j = l // 2
        last = l == depth - 1
        if l % 2 == 0:
            xn = _rmsnorm(x2, g_ffn_norm[l], jnp.bfloat16, name="norm_ffn")
            h = _swiglu_up("dense_up", xn, _bf16(w_dense_gate[j]), _bf16(w_dense_up[j]))
            x2 = _linear("ffn_down", h, _bf16(w_dense_down[j]), jnp.float32, epilogue=_ep_residual,
                         extras=[(x2, "tile")])
            if last:
                x2 = _rmsnorm(x2, g_final, x.dtype, name="norm_final")
        else:
            x2 = _moe_ffn(x2, g_ffn_norm[l], w_router[j], w_exp_gate[j], w_exp_up[j], w_exp_down[j],
                          g_final if last else None)
    return x2.reshape(batch, seq, d)
```

```python
import functools

import jax
import jax.numpy as jnp
from jax import lax
from jax.experimental import pallas as pl
from jax.experimental.pallas import tpu as pltpu

RMS_EPS = 1e-6
MLA_HEADS = 16
Q_LORA = 1536
KV_LORA = 512
NOPE_DIM = 128
ROPE_DIM = 64
MLA_V_DIM = 128
ROPE_THETA = 10000.0
FOX_HEADS = 16
FOX_HEAD_DIM = 128
SWA_Q_HEADS = 32
SWA_KV_HEADS = 4
SWA_HEAD_DIM = 64
WINDOW = 128
N_BRANCHES = 3
N_EXPERTS = 8
TOP_K = 2

LANE = 128
V7X_VMEM_BYTES = 64 * 1024 * 1024
V7X_VMEM_BUDGET = 56 * 1024 * 1024
COMPILER_SCRATCH_BYTES = 12 * 1024 * 1024

MM_BM = 1024
MM_BN = 1024
MM_BK = 2048
MM_FULL_K = 4096
MM_BN_MULTI = 512
NORM_BM = 512
FLASH_BQ = 512
FLASH_BK = 1024
FLASH_HEADS_PER_STEP = 2
CUM_BLOCK = 512
MLA_KV_BM = 512
ROUTER_BM = 512
MOE_ROW_TILE = 512
MOE_COMBINE_TILE = 256
DMA_ISSUE_UNROLL = 8

MLA_QK = NOPE_DIM + 2 * ROPE_DIM
LOG2E = 1.4426950408889634


def _params(semantics, vmem_bytes):
    limit = min(V7X_VMEM_BUDGET, vmem_bytes + COMPILER_SCRATCH_BYTES)
    return pltpu.CompilerParams(dimension_semantics=semantics, vmem_limit_bytes=int(limit))


def _nbytes(shape, dtype):
    n = 1
    for s in shape:
        n *= s
    return n * jnp.dtype(dtype).itemsize


def _tile(full, want):
    t = min(full, want)
    while full % t:
        t //= 2
    return t


def _rmsnorm_body(x_ref, g_ref, o_ref):
    x = x_ref[...].astype(jnp.float32)
    y = x * lax.rsqrt(jnp.mean(x * x, axis=-1, keepdims=True) + RMS_EPS)
    o_ref[...] = (y * g_ref[...]).astype(o_ref.dtype)


def _rmsnorm(x, g, out_dtype, *, col_block=0, width=None, name="rmsnorm"):
    rows = x.shape[0]
    width = x.shape[1] if width is None else width
    bm = _tile(rows, NORM_BM)
    vmem = 2 * _nbytes((bm, width), x.dtype) + 2 * _nbytes((bm, width), out_dtype)
    return pl.pallas_call(
        _rmsnorm_body,
        grid=(rows // bm,),
        in_specs=[pl.BlockSpec((bm, width), lambda i: (i, col_block)),
                  pl.BlockSpec((1, width), lambda i: (0, 0))],
        out_specs=pl.BlockSpec((bm, width), lambda i: (i, 0)),
        out_shape=jax.ShapeDtypeStruct((rows, width), out_dtype),
        compiler_params=_params(("parallel",), vmem),
        name=name,
    )(x, g.reshape(1, width).astype(jnp.float32))


def _mm_body(n_a, n_w, n_x, dots, nk, epilogue, *refs):
    a = refs[:n_a]
    w = refs[n_a:n_a + n_w]
    xs = refs[n_a + n_w:n_a + n_w + n_x]
    o = refs[n_a + n_w + n_x]
    accs = refs[n_a + n_w + n_x + 1:]

    def products():
        return [jnp.dot(a[ai][...], w[wi][...], preferred_element_type=jnp.float32)
                for ai, wi in dots]

    def finish(vals):
        o[...] = epilogue(vals, *[x[...] for x in xs]).astype(o.dtype)

    if nk == 1:
        finish(products())
        return

    k = pl.program_id(2)

    @pl.when(k == 0)
    def _():
        for acc, p in zip(accs, products()):
            acc[...] = p

    @pl.when((k > 0) & (k < nk - 1))
    def _():
        for acc, p in zip(accs, products()):
            acc[...] += p

    @pl.when(k == nk - 1)
    def _():
        finish([acc[...] + p for acc, p in zip(accs, products())])


def _mm(name, a_ops, w_ops, x_ops, dots, epilogue, *, m, n, bm, bn, nk, out_dtype):
    ops = list(a_ops) + list(w_ops) + list(x_ops)
    vmem = 2 * _nbytes((bm, bn), out_dtype)
    for arr, spec in ops:
        blk = [d for d in spec.block_shape if d is not None]
        vmem += 2 * _nbytes(blk, arr.dtype)
    scratch = []
    if nk > 1:
        scratch = [pltpu.VMEM((bm, bn), jnp.float32) for _ in dots]
        vmem += len(dots) * _nbytes((bm, bn), jnp.float32)
    body = functools.partial(_mm_body, len(a_ops), len(w_ops), len(x_ops), tuple(dots), nk, epilogue)
    return pl.pallas_call(
        body,
        grid=(m // bm, n // bn, nk),
        in_specs=[spec for _, spec in ops],
        out_specs=pl.BlockSpec((bm, bn), lambda i, j, k: (i, j)),
        out_shape=jax.ShapeDtypeStruct((m, n), out_dtype),
        scratch_shapes=scratch,
        compiler_params=_params(("parallel", "parallel", "arbitrary"), vmem),
        name=name,
    )(*[arr for arr, _ in ops])


def _a_spec(bm, bk):
    return pl.BlockSpec((bm, bk), lambda i, j, k: (i, k))


def _w_spec(bk, bn):
    return pl.BlockSpec((bk, bn), lambda i, j, k: (k, j))


def _tile_spec(bm, bn, col_off=0):
    return pl.BlockSpec((bm, bn), lambda i, j, k: (i, j + col_off))


def _row_spec(bn, col_off=0):
    return pl.BlockSpec((1, bn), lambda i, j, k: (0, j + col_off))


def _ep_plain(vals):
    return vals[0]


def _ep_residual(vals, res):
    return res + vals[0]


def _ep_sigmoid_bias(vals, bias):
    return jax.nn.sigmoid(vals[0] + bias)


def _ep_scale_leading(n_blocks, factor, vals):
    return vals[0] * jnp.where(pl.program_id(1) < n_blocks, factor, 1.0)


def _ep_swiglu(vals):
    return jax.nn.silu(vals[0]) * vals[1]


def _ep_merge(vals, g0, g1, g2):
    return (g0.astype(jnp.float32) * vals[0] + g1.astype(jnp.float32) * vals[1]
            + g2.astype(jnp.float32) * vals[2])


def _linear(name, a, w, out_dtype, *, epilogue=_ep_plain, extras=(), bn=MM_BN):
    m, kdim = a.shape
    n = w.shape[1]
    bm, bn, bk = _tile(m, MM_BM), _tile(n, bn), _tile(kdim, MM_BK if kdim > MM_FULL_K else kdim)
    x_ops = []
    for arr, kind in extras:
        x_ops.append((arr, _tile_spec(bm, bn) if kind == "tile" else _row_spec(bn)))
    return _mm(name, [(a, _a_spec(bm, bk))], [(w, _w_spec(bk, bn))], x_ops, [(0, 0)], epilogue,
               m=m, n=n, bm=bm, bn=bn, nk=kdim // bk, out_dtype=out_dtype)


def _rope_tile(t, cos_t, sin_a, sin_b):
    half = ROPE_DIM // 2
    return (t * cos_t + pltpu.roll(t, LANE - half, axis=1) * sin_a
            + pltpu.roll(t, half, axis=1) * sin_b)


def _rope_tables(seq):
    half = ROPE_DIM // 2
    inv = ROPE_THETA ** (-jnp.arange(half, dtype=jnp.float32) / half)
    ang = jnp.arange(seq, dtype=jnp.float32)[:, None] * inv[None, :]
    cos, sin = jnp.cos(ang), jnp.sin(ang)
    z = jnp.zeros_like(cos)
    pad = jnp.zeros((seq, LANE - 2 * half), jnp.float32)
    cos_t = jnp.concatenate([cos, cos, pad], axis=1)
    sin_a = jnp.concatenate([-sin, z, pad], axis=1)
    sin_b = jnp.concatenate([z, sin, pad], axis=1)
    return cos_t, sin_a, sin_b


def _ep_mla_q(vals, cos_t, sin_a, sin_b):
    acc = vals[0] * ((NOPE_DIM + ROPE_DIM) ** -0.5 * LOG2E)
    pieces = []
    for h in range(acc.shape[1] // MLA_QK):
        lo = h * MLA_QK
        pieces.append(acc[:, lo:lo + NOPE_DIM])
        pieces.append(_rope_tile(acc[:, lo + NOPE_DIM:lo + MLA_QK], cos_t, sin_a, sin_b))
    return jnp.concatenate(pieces, axis=1)


def _mla_q(cq_n, w_uq_p, tables, seq):
    m, kdim = cq_n.shape
    n = w_uq_p.shape[1]
    bm = _tile(seq, MM_BM)
    bn = _tile(n, MM_BN)
    nseq = seq // bm
    tab_spec = pl.BlockSpec((bm, LANE), lambda i, j, k: (i % nseq, 0))
    return _mm("mla_q", [(cq_n, _a_spec(bm, kdim))], [(w_uq_p, _w_spec(kdim, bn))],
               [(t, tab_spec) for t in tables], [(0, 0)], _ep_mla_q,
               m=m, n=n, bm=bm, bn=bn, nk=1, out_dtype=jnp.bfloat16)


def _mla_kv_body(heads, c_ref, g_ref, wk_ref, wv_ref, kr_ref, cos_ref, sina_ref, sinb_ref, k_ref, v_ref):
    c = c_ref[...]
    a = (c * lax.rsqrt(jnp.mean(c * c, axis=-1, keepdims=True) + RMS_EPS) * g_ref[...]).astype(wk_ref.dtype)
    kn = jnp.dot(a, wk_ref[...], preferred_element_type=jnp.float32)
    v_ref[...] = jnp.dot(a, wv_ref[...], preferred_element_type=jnp.float32).astype(v_ref.dtype)
    kr = _rope_tile(kr_ref[...], cos_ref[...], sina_ref[...], sinb_ref[...]).astype(k_ref.dtype)
    for h in range(heads):
        k_ref[:, h * MLA_QK:h * MLA_QK + NOPE_DIM] = kn[:, h * NOPE_DIM:(h + 1) * NOPE_DIM].astype(k_ref.dtype)
        k_ref[:, h * MLA_QK + NOPE_DIM:(h + 1) * MLA_QK] = kr


def _mla_kv(lat, ckv_col_block, g_kv, w_k, w_v, kr_col_block, tables, seq):
    m = lat.shape[0]
    kdim = w_k.shape[0]
    heads = w_k.shape[1] // NOPE_DIM
    bm = _tile(seq, MLA_KV_BM)
    nseq = seq // bm
    tab_spec = pl.BlockSpec((bm, LANE), lambda i: (i % nseq, 0))
    nk_out, nv_out = heads * MLA_QK, heads * MLA_V_DIM
    vmem = (2 * _nbytes((bm, kdim), lat.dtype) + 2 * _nbytes(w_k.shape, w_k.dtype)
            + 2 * _nbytes(w_v.shape, w_v.dtype) + 8 * _nbytes((bm, LANE), jnp.float32)
            + 2 * _nbytes((bm, nk_out), jnp.bfloat16) + 2 * _nbytes((bm, nv_out), jnp.bfloat16))
    return pl.pallas_call(
        functools.partial(_mla_kv_body, heads),
        grid=(m // bm,),
        in_specs=[pl.BlockSpec((bm, kdim), lambda i: (i, ckv_col_block)),
                  pl.BlockSpec((1, kdim), lambda i: (0, 0)),
                  pl.BlockSpec(w_k.shape, lambda i: (0, 0)),
                  pl.BlockSpec(w_v.shape, lambda i: (0, 0)),
                  pl.BlockSpec((bm, LANE), lambda i: (i, kr_col_block)),
                  tab_spec, tab_spec, tab_spec],
        out_specs=[pl.BlockSpec((bm, nk_out), lambda i: (i, 0)),
                   pl.BlockSpec((bm, nv_out), lambda i: (i, 0))],
        out_shape=[jax.ShapeDtypeStruct((m, nk_out), jnp.bfloat16),
                   jax.ShapeDtypeStruct((m, nv_out), jnp.bfloat16)],
        compiler_params=_params(("parallel",), vmem),
        name="mla_kv",
    )(lat, g_kv.reshape(1, kdim).astype(jnp.float32), w_k, w_v, lat, *tables)


def _split_bf16(x):
    hi = x.astype(jnp.bfloat16)
    r = x - hi.astype(jnp.float32)
    mid = r.astype(jnp.bfloat16)
    lo = (r - mid.astype(jnp.float32)).astype(jnp.bfloat16)
    return hi, mid, lo


def _cum_body(logit_ref, b_ref, o_ref, carry_ref):
    @pl.when(pl.program_id(1) == 0)
    def _():
        carry_ref[...] = jnp.zeros_like(carry_ref)

    z = logit_ref[...] + b_ref[...]
    log_f = (jnp.minimum(z, 0.0) - jnp.log1p(jnp.exp(-jnp.abs(z)))) * LOG2E
    n = log_f.shape[0]
    tri = (lax.broadcasted_iota(jnp.int32, (n, n), 0)
           >= lax.broadcasted_iota(jnp.int32, (n, n), 1)).astype(jnp.bfloat16)
    cum = carry_ref[...]
    for part in _split_bf16(log_f):
        cum = cum + jnp.dot(tri, part, preferred_element_type=jnp.float32)
    o_ref[...] = cum
    carry_ref[...] = cum[n - 1:n, :]


def _forget_cumsum(logit, col_block, b_pad, batch, seq):
    bm = _tile(seq, CUM_BLOCK)
    ns = seq // bm
    vmem = 4 * _nbytes((bm, LANE), jnp.float32) + _nbytes((bm, bm), jnp.float32)
    return pl.pallas_call(
        _cum_body,
        grid=(batch, ns),
        in_specs=[pl.BlockSpec((bm, LANE), lambda b, s: (b * ns + s, col_block)),
                  pl.BlockSpec((1, LANE), lambda b, s: (0, 0))],
        out_specs=pl.BlockSpec((bm, LANE), lambda b, s: (b * ns + s, 0)),
        out_shape=jax.ShapeDtypeStruct((logit.shape[0], LANE), jnp.float32),
        scratch_shapes=[pltpu.VMEM((1, LANE), jnp.float32)],
        compiler_params=_params(("parallel", "arbitrary"), vmem),
        name="forget_cumsum",
    )(logit, b_pad)


def _flash_body(bq, bk, hp, dk, dv, with_cum, *refs):
    if with_cum:
        q_ref, k_ref, v_ref, cc_ref, cr_ref, o_ref = refs
    else:
        q_ref, k_ref, v_ref, o_ref = refs
    i = pl.program_id(2)
    wide = bk // bq
    qs = [q_ref[:, h * dk:(h + 1) * dk] for h in range(hp)]
    if with_cum:
        head0 = pl.program_id(1) * hp
        cc = cc_ref[...]
        lane = lax.broadcasted_iota(jnp.int32, cc.shape, 1)
        cum_q = [jnp.sum(jnp.where(lane == head0 + h, cc, 0.0), axis=1, keepdims=True)
                 for h in range(hp)]

    def step(j, carry, diag_offset=None):
        start = pl.multiple_of(j * bk, bk)
        diagonal = diag_offset is not None
        if diagonal:
            causal = (lax.broadcasted_iota(jnp.int32, (bq, bk), 0) + diag_offset
                      >= lax.broadcasted_iota(jnp.int32, (bq, bk), 1))
        out = []
        for h in range(hp):
            m, l, acc = carry[h]
            k = k_ref[pl.ds(start, bk), h * dk:(h + 1) * dk]
            v = v_ref[pl.ds(start, bk), h * dv:(h + 1) * dv]
            s = lax.dot_general(qs[h], k, (((1,), (1,)), ((), ())), preferred_element_type=jnp.float32)
            if with_cum:
                s = s - cr_ref[h, j]
            if diagonal:
                s = jnp.where(causal, s, -jnp.inf)
            row_max = jnp.max(s, axis=1, keepdims=True)
            if with_cum:
                m_new = jnp.maximum(m, row_max + cum_q[h])
                shift = m_new - cum_q[h]
            else:
                m_new = jnp.maximum(m, row_max)
                shift = m_new
            alpha = jnp.exp2(m - m_new)
            p = jnp.exp2(s - shift)
            l = alpha * l + jnp.sum(p, axis=1, keepdims=True)
            acc = alpha * acc + jnp.dot(p.astype(v.dtype), v, preferred_element_type=jnp.float32)
            out.append((m_new, l, acc))
        return tuple(out)

    init = tuple((jnp.full((bq, 1), -jnp.inf, jnp.float32), jnp.zeros((bq, 1), jnp.float32),
                  jnp.zeros((bq, dv), jnp.float32)) for _ in range(hp))
    n_wide = lax.div(i, wide)
    carry = lax.fori_loop(0, n_wide, step, init)
    final = step(n_wide, carry, lax.rem(i, wide) * bq)
    for h in range(hp):
        _, l, acc = final[h]
        o_ref[:, h * dv:(h + 1) * dv] = (acc / l).astype(o_ref.dtype)


def _flash(name, q_arr, k_arr, v_arr, *, q_col0, k_col0, v_col0, dk, dv, heads, batch, seq,
           cum_col=None, cum_row=None):
    bq = _tile(seq, FLASH_BQ)
    bk = _tile(seq, max(FLASH_BK, bq))
    assert bk % bq == 0
    nq, nk = seq // bq, seq // bk
    hp = FLASH_HEADS_PER_STEP
    assert heads % hp == 0 and q_col0 % hp == 0 and k_col0 % hp == 0 and v_col0 % hp == 0
    with_cum = cum_col is not None
    in_specs = [pl.BlockSpec((bq, hp * dk), lambda b, g, i: (b * nq + i, q_col0 // hp + g)),
                pl.BlockSpec((seq, hp * dk), lambda b, g, i: (b, k_col0 // hp + g)),
                pl.BlockSpec((seq, hp * dv), lambda b, g, i: (b, v_col0 // hp + g))]
    args = [q_arr, k_arr, v_arr]
    vmem = (2 * _nbytes((bq, hp * dk), q_arr.dtype) + 2 * _nbytes((seq, hp * dk), k_arr.dtype)
            + 2 * _nbytes((seq, hp * dv), v_arr.dtype) + 2 * _nbytes((bq, hp * dv), jnp.bfloat16))
    if with_cum:
        in_specs += [pl.BlockSpec((bq, LANE), lambda b, g, i: (b * nq + i, 0)),
                     pl.BlockSpec((hp, nk, 1, bk), lambda b, g, i: (b * (heads // hp) + g, 0, 0, 0))]
        args += [cum_col, cum_row.reshape(-1, nk, 1, bk)]
        vmem += 2 * _nbytes((bq, LANE), jnp.float32) + 2 * _nbytes((hp, nk, 8, bk), jnp.float32)
    return pl.pallas_call(
        functools.partial(_flash_body, bq, bk, hp, dk, dv, with_cum),
        grid=(batch, heads // hp, nq),
        in_specs=in_specs,
        out_specs=pl.BlockSpec((bq, hp * dv), lambda b, g, i: (b * nq + i, g)),
        out_shape=jax.ShapeDtypeStruct((batch * seq, heads * dv), jnp.bfloat16),
        compiler_params=_params(("parallel", "parallel", "arbitrary"), vmem),
        name=name,
    )(*args)


def _block_diag_pair(slab, head_in_slab):
    d = SWA_HEAD_DIM
    lane = lax.broadcasted_iota(jnp.int32, slab.shape, 1)
    x32 = slab.astype(jnp.float32)
    if head_in_slab == 0:
        left = jnp.where(lane < d, x32, 0.0)
        right = pltpu.roll(left, d, axis=1)
    else:
        right = jnp.where(lane >= d, x32, 0.0)
        left = pltpu.roll(right, d, axis=1)
    return jnp.concatenate([left, right], axis=0).astype(slab.dtype)


def _swa_body(sink_ref, q_ref, kp_ref, kc_ref, vp_ref, vc_ref, o_ref):
    n = pl.program_id(1)
    blk = q_ref.shape[0]
    group = SWA_Q_HEADS // SWA_KV_HEADS
    d = SWA_HEAD_DIM
    keys = 2 * blk
    row = lax.broadcasted_iota(jnp.int32, (blk, keys), 0)
    col = lax.broadcasted_iota(jnp.int32, (blk, keys), 1)
    dist = row + blk - col
    valid = (dist >= 0) & (dist < WINDOW) & ((col >= blk) | (n > 0))
    dist_f = dist.astype(jnp.float32)
    out_lane = lax.broadcasted_iota(jnp.int32, (blk, 2 * d), 1)
    for g in range(SWA_KV_HEADS):
        slab = slice((g // 2) * 2 * d, (g // 2 + 1) * 2 * d)
        k2 = _block_diag_pair(jnp.concatenate([kp_ref[:, slab], kc_ref[:, slab]], axis=0), g % 2)
        v2 = _block_diag_pair(jnp.concatenate([vp_ref[:, slab], vc_ref[:, slab]], axis=0), g % 2)
        pairs = list(range(0, group, 2))
        q_all = jnp.concatenate([q_ref[:, (g * group + u) * d:(g * group + u + 2) * d] for u in pairs], axis=0)
        s_all = lax.dot_general(q_all, k2, (((1,), (1,)), ((), ())), preferred_element_type=jnp.float32)
        p_rows, den_rows = [], []
        for t, u in enumerate(pairs):
            h = g * group + u
            slopes = [LOG2E * 2.0 ** (-8.0 * (h + c + 1) / SWA_Q_HEADS) for c in range(2)]
            sinks = [sink_ref[h + c] * LOG2E for c in range(2)]
            s = s_all[t * blk:(t + 1) * blk]
            ps, denoms = [], []
            for c in range(2):
                s_c = jnp.where(valid, s[:, c * keys:(c + 1) * keys] - slopes[c] * dist_f, -jnp.inf)
                m = jnp.maximum(jnp.max(s_c, axis=1, keepdims=True), sinks[c])
                p = jnp.exp2(s_c - m)
                ps.append(p)
                denoms.append(jnp.sum(p, axis=1, keepdims=True) + jnp.exp2(sinks[c] - m))
            p_rows.append(jnp.concatenate(ps, axis=1).astype(v2.dtype))
            den_rows.append(jnp.where(out_lane < d, denoms[0], denoms[1]))
        o_all = jnp.dot(jnp.concatenate(p_rows, axis=0), v2, preferred_element_type=jnp.float32)
        for t, u in enumerate(pairs):
            h = g * group + u
            o_ref[:, h * d:(h + 2) * d] = (o_all[t * blk:(t + 1) * blk] / den_rows[t]).astype(o_ref.dtype)


def _swa(sqkv, sinks, batch, seq):
    blk = WINDOW
    nb = seq // blk
    qw = SWA_Q_HEADS * SWA_HEAD_DIM
    kvw = SWA_KV_HEADS * SWA_HEAD_DIM
    k_col, v_col = qw // kvw, qw // kvw + 1
    cur = lambda col: pl.BlockSpec((blk, kvw), lambda b, n: (b * nb + n, col))
    prev = lambda col: pl.BlockSpec((blk, kvw), lambda b, n: (b * nb + jnp.maximum(n - 1, 0), col))
    vmem = 4 * _nbytes((blk, qw), jnp.bfloat16) + 8 * _nbytes((blk, kvw), jnp.bfloat16)
    return pl.pallas_call(
        _swa_body,
        grid=(batch, nb),
        in_specs=[pl.BlockSpec(memory_space=pltpu.SMEM),
                  pl.BlockSpec((blk, qw), lambda b, n: (b * nb + n, 0)),
                  prev(k_col), cur(k_col), prev(v_col), cur(v_col)],
        out_specs=pl.BlockSpec((blk, qw), lambda b, n: (b * nb + n, 0)),
        out_shape=jax.ShapeDtypeStruct((batch * seq, qw), jnp.bfloat16),
        compiler_params=_params(("parallel", "parallel"), vmem),
        name="swa",
    )(sinks.astype(jnp.float32), sqkv, sqkv, sqkv, sqkv, sqkv)


def _router_body(x_ref, g_ref, w_ref, eid_ref, rank_ref, wts_ref, cnt_ref, xn_ref, base_ref):
    @pl.when(pl.program_id(0) == 0)
    def _():
        base_ref[...] = jnp.zeros_like(base_ref)

    x = x_ref[...]
    xn = x * lax.rsqrt(jnp.mean(x * x, axis=-1, keepdims=True) + RMS_EPS) * g_ref[...]
    xn_ref[...] = _pack_bf16_pairs(xn)
    logits = jnp.dot(xn, w_ref[...], preferred_element_type=jnp.float32,
                     precision=lax.Precision.HIGHEST)
    lane = lax.broadcasted_iota(jnp.int32, logits.shape, 1)
    logits = jnp.where(lane < N_EXPERTS, logits, -jnp.inf)
    top1 = jnp.max(logits, axis=1, keepdims=True)
    idx1 = jnp.min(jnp.where(logits == top1, lane, LANE), axis=1, keepdims=True)
    rest = jnp.where(lane == idx1, -jnp.inf, logits)
    top2 = jnp.max(rest, axis=1, keepdims=True)
    idx2 = jnp.min(jnp.where(rest == top2, lane, LANE), axis=1, keepdims=True)
    e2 = jnp.exp(top2 - top1)
    w1 = 1.0 / (1.0 + e2)
    w2 = e2 / (1.0 + e2)
    n = x.shape[0]
    hot = (lane == idx1) | (lane == idx2)
    earlier = (lax.broadcasted_iota(jnp.int32, (n, n), 0)
               > lax.broadcasted_iota(jnp.int32, (n, n), 1)).astype(jnp.bfloat16)
    before = base_ref[...] + jnp.dot(earlier, hot.astype(jnp.bfloat16), preferred_element_type=jnp.float32)
    rank1 = jnp.sum(jnp.where(lane == idx1, before, 0.0), axis=1, keepdims=True)
    rank2 = jnp.sum(jnp.where(lane == idx2, before, 0.0), axis=1, keepdims=True)
    eid_ref[...] = jnp.where(lane == 0, idx1, jnp.where(lane == 1, idx2, 0))
    rank_ref[...] = jnp.where(lane == 0, rank1, jnp.where(lane == 1, rank2, 0.0)).astype(jnp.int32)
    wts_ref[...] = jnp.where(lane == 0, w1, jnp.where(lane == 1, w2, 0.0))
    total = base_ref[...] + jnp.sum(hot.astype(jnp.float32), axis=0, keepdims=True)
    base_ref[...] = total
    cnt_ref[...] = total.astype(jnp.int32)


def _router(x, g, w_router):
    rows, d = x.shape
    bm = _tile(rows, ROUTER_BM)
    w_pad = jnp.pad(w_router.astype(jnp.float32), ((0, 0), (0, LANE - w_router.shape[1])))
    vmem = (2 * _nbytes((bm, d), jnp.float32) + 2 * _nbytes((d, LANE), jnp.float32)
            + 6 * _nbytes((bm, LANE), jnp.float32) + _nbytes((bm, bm), jnp.float32)
            + 2 * _nbytes((bm, d // 2), jnp.uint32))
    tile = pl.BlockSpec((bm, LANE), lambda i: (i, 0))
    return pl.pallas_call(
        _router_body,
        grid=(rows // bm,),
        in_specs=[pl.BlockSpec((bm, d), lambda i: (i, 0)),
                  pl.BlockSpec((1, d), lambda i: (0, 0)),
                  pl.BlockSpec((d, LANE), lambda i: (0, 0))],
        out_specs=[tile, tile, tile, pl.BlockSpec((1, LANE), lambda i: (0, 0)),
                   pl.BlockSpec((bm, d // 2), lambda i: (i, 0))],
        out_shape=[jax.ShapeDtypeStruct((rows, LANE), jnp.int32),
                   jax.ShapeDtypeStruct((rows, LANE), jnp.int32),
                   jax.ShapeDtypeStruct((rows, LANE), jnp.float32),
                   jax.ShapeDtypeStruct((1, LANE), jnp.int32),
                   jax.ShapeDtypeStruct((rows, d // 2), jnp.uint32)],
        scratch_shapes=[pltpu.VMEM((1, LANE), jnp.float32)],
        compiler_params=_params(("arbitrary",), vmem),
        name="router",
    )(x, g.reshape(1, d).astype(jnp.float32), w_pad)


def _moe_plan(eid, rank, counts, bm, n_tiles):
    cnt = counts[0, :N_EXPERTS]
    padded = (cnt + bm - 1) // bm * bm
    ends = jnp.cumsum(padded)
    starts = ends - padded
    e_sel = eid[:, :TOP_K]
    start_sel = jnp.zeros_like(e_sel)
    for e in range(N_EXPERTS):
        start_sel = jnp.where(e_sel == e, starts[e], start_sel)
    slot = (start_sel + rank[:, :TOP_K]).astype(jnp.int32)
    tokens = jnp.repeat(jnp.arange(slot.shape[0], dtype=jnp.int32), TOP_K)
    src = jnp.zeros((n_tiles * bm,), jnp.int32).at[slot.reshape(-1)].set(tokens, unique_indices=True)
    tile_start = jnp.arange(n_tiles, dtype=jnp.int32) * bm
    tile_expert = jnp.minimum(jnp.sum(tile_start[:, None] >= ends[None, :], axis=1), N_EXPERTS - 1)
    n_used = (ends[-1] // bm).reshape(1)
    return slot, src, tile_expert.astype(jnp.int32), n_used.astype(jnp.int32)


def _pack_bf16_pairs(y):
    half = y.shape[1] // 2
    bits = lax.bitcast_convert_type(y.astype(jnp.bfloat16).astype(jnp.float32), jnp.uint32)
    return bits[:, half:] | (bits[:, :half] >> 16)


def _unpack_bf16_pairs(words):
    lo = lax.bitcast_convert_type(words << 16, jnp.float32).astype(jnp.bfloat16)
    hi = lax.bitcast_convert_type(words & jnp.uint32(0xFFFF0000), jnp.float32).astype(jnp.bfloat16)
    return lo, hi


def _gather_body(src_ref, src_next_ref, rows_hbm, o_ref, buf, sem):
    i = pl.program_id(0)
    n_steps = pl.num_programs(0)
    bt = o_ref.shape[0]

    def row_copy(src, half, r):
        return pltpu.make_async_copy(rows_hbm.at[pl.ds(src[0, r], 1)], buf.at[half, pl.ds(r, 1)],
                                     sem.at[half])

    def start_all(src, half):
        def body(r, _):
            row_copy(src, half, r).start()
            return 0
        lax.fori_loop(0, bt, body, 0, unroll=DMA_ISSUE_UNROLL)

    def wait_all(src, half):
        def body(r, _):
            row_copy(src, half, r).wait()
            return 0
        lax.fori_loop(0, bt, body, 0, unroll=DMA_ISSUE_UNROLL)

    cur = i % 2

    @pl.when(i == 0)
    def _():
        start_all(src_ref, 0)

    @pl.when(i + 1 < n_steps)
    def _():
        start_all(src_next_ref, 1 - cur)

    wait_all(src_ref, cur)
    o_ref[...] = buf[cur]


def _moe_gather(rows, src, bt):
    n_sorted = src.shape[0]
    width = rows.shape[1]
    n_steps = n_sorted // bt
    src_blk = src.reshape(n_steps, 1, bt)
    src_spec = lambda f: pl.BlockSpec((None, 1, bt), lambda i: (f(i), 0, 0), memory_space=pltpu.SMEM)
    vmem = 4 * _nbytes((bt, width), rows.dtype)
    return pl.pallas_call(
        _gather_body,
        grid=(n_steps,),
        in_specs=[src_spec(lambda i: i), src_spec(lambda i: jnp.minimum(i + 1, n_steps - 1)),
                  pl.BlockSpec(memory_space=pl.ANY)],
        out_specs=pl.BlockSpec((bt, width), lambda i: (i, 0)),
        out_shape=jax.ShapeDtypeStruct((n_sorted, width), rows.dtype),
        scratch_shapes=[pltpu.VMEM((2, bt, width), rows.dtype), pltpu.SemaphoreType.DMA((2,))],
        compiler_params=_params(("arbitrary",), vmem),
        name="moe_gather",
    )(src_blk, src_blk, rows)


def _grouped_body(packed, swiglu, te_ref, nu_ref, a_ref, *refs):
    del te_ref
    w_refs, o_ref = refs[:-1], refs[-1]
    i = pl.program_id(1)

    @pl.when(i < nu_ref[0])
    def _():
        if packed:
            lo, hi = _unpack_bf16_pairs(a_ref[...])
            half = lo.shape[1]
            prods = [jnp.dot(lo, w[pl.ds(0, half), :], preferred_element_type=jnp.float32)
                     + jnp.dot(hi, w[pl.ds(half, half), :], preferred_element_type=jnp.float32)
                     for w in w_refs]
        else:
            a = a_ref[...]
            prods = [jnp.dot(a, w[...], preferred_element_type=jnp.float32) for w in w_refs]
        o_ref[...] = (_ep_swiglu(prods) if swiglu else prods[0]).astype(o_ref.dtype)

    @pl.when(i >= nu_ref[0])
    def _():
        o_ref[...] = jnp.zeros_like(o_ref)


def _grouped_mm(name, a, weights, tile_expert, n_used, *, bm, packed, swiglu, out_dtype):
    m = a.shape[0]
    _, kdim, n = weights[0].shape
    bn = _tile(n, MM_BN_MULTI if len(weights) > 1 else MM_BN)
    w_spec = pl.BlockSpec((None, kdim, bn), lambda j, i, te, nu: (te[i], 0, j))
    vmem = (2 * _nbytes((bm, a.shape[1]), a.dtype) + 2 * len(weights) * _nbytes((kdim, bn), weights[0].dtype)
            + 2 * _nbytes((bm, bn), out_dtype) + (_nbytes((bm, kdim), jnp.bfloat16) if packed else 0))
    grid_spec = pltpu.PrefetchScalarGridSpec(
        num_scalar_prefetch=2,
        grid=(n // bn, m // bm),
        in_specs=[pl.BlockSpec((bm, a.shape[1]), lambda j, i, te, nu: (i, 0))] + [w_spec] * len(weights),
        out_specs=pl.BlockSpec((bm, bn), lambda j, i, te, nu: (i, j)),
    )
    return pl.pallas_call(
        functools.partial(_grouped_body, packed, swiglu),
        grid_spec=grid_spec,
        out_shape=jax.ShapeDtypeStruct((m, n), out_dtype),
        compiler_params=_params(("parallel", "arbitrary"), vmem),
        name=name,
    )(tile_expert, n_used, a, *weights)


def _combine_body(with_norm, slot_ref, slot_next_ref, x_ref, wts_ref, *refs):
    if with_norm:
        g_ref, y_hbm, o_ref, buf, sem = refs
    else:
        y_hbm, o_ref, buf, sem = refs
    i = pl.program_id(0)
    n_steps = pl.num_programs(0)
    bt = x_ref.shape[0]

    def row_copy(slots, half, r, c):
        return pltpu.make_async_copy(y_hbm.at[pl.ds(slots[0, r * TOP_K + c], 1)],
                                     buf.at[half, c, pl.ds(r, 1)], sem.at[half])

    def start_all(slots, half):
        def body(r, _):
            for c in range(TOP_K):
                row_copy(slots, half, r, c).start()
            return 0
        lax.fori_loop(0, bt, body, 0, unroll=DMA_ISSUE_UNROLL)

    def wait_all(slots, half):
        def body(r, _):
            for c in range(TOP_K):
                row_copy(slots, half, r, c).wait()
            return 0
        lax.fori_loop(0, bt, body, 0, unroll=DMA_ISSUE_UNROLL)

    cur = i % 2

    @pl.when(i == 0)
    def _():
        start_all(slot_ref, 0)

    @pl.when(i + 1 < n_steps)
    def _():
        start_all(slot_next_ref, 1 - cur)

    wait_all(slot_ref, cur)
    wts = wts_ref[...]
    lane = lax.broadcasted_iota(jnp.int32, wts.shape, 1)
    out = x_ref[...]
    for c in range(TOP_K):
        w_c = jnp.sum(jnp.where(lane == c, wts, 0.0), axis=1, keepdims=True)
        out = out + w_c * buf[cur, c]
    if with_norm:
        out = out * lax.rsqrt(jnp.mean(out * out, axis=-1, keepdims=True) + RMS_EPS) * g_ref[...]
    o_ref[...] = out.astype(o_ref.dtype)


def _moe_combine(x, y_sorted, slot, wts, g_final=None):
    t, d = x.shape
    bt = _tile(t, MOE_COMBINE_TILE)
    n_steps = t // bt
    slot_blk = slot.reshape(n_steps, 1, bt * TOP_K)
    slot_spec = lambda f: pl.BlockSpec((None, 1, bt * TOP_K), lambda i: (f(i), 0, 0), memory_space=pltpu.SMEM)
    in_specs = [slot_spec(lambda i: i), slot_spec(lambda i: jnp.minimum(i + 1, n_steps - 1)),
                pl.BlockSpec((bt, d), lambda i: (i, 0)), pl.BlockSpec((bt, LANE), lambda i: (i, 0))]
    args = [slot_blk, slot_blk, x, wts]
    if g_final is not None:
        in_specs.append(pl.BlockSpec((1, d), lambda i: (0, 0)))
        args.append(g_final.reshape(1, d).astype(jnp.float32))
    in_specs.append(pl.BlockSpec(memory_space=pl.ANY))
    args.append(y_sorted)
    vmem = (4 * _nbytes((bt, d), jnp.float32) + 2 * _nbytes((bt, LANE), jnp.float32)
            + 2 * TOP_K * _nbytes((bt, d), y_sorted.dtype))
    return pl.pallas_call(
        functools.partial(_combine_body, g_final is not None),
        grid=(n_steps,),
        in_specs=in_specs,
        out_specs=pl.BlockSpec((bt, d), lambda i: (i, 0)),
        out_shape=jax.ShapeDtypeStruct((t, d), x.dtype),
        scratch_shapes=[pltpu.VMEM((2, TOP_K, bt, d), y_sorted.dtype), pltpu.SemaphoreType.DMA((2,))],
        compiler_params=_params(("arbitrary",), vmem),
        name="moe_combine",
    )(*args)


def _moe_ffn(x2, g_ffn, w_router, w_gate, w_up, w_down, g_final=None):
    t, d = x2.shape
    bm = _tile(t, MOE_ROW_TILE)
    n_sorted = TOP_K * t + N_EXPERTS * bm
    eid, rank, wts, counts, xn_packed = _router(x2, g_ffn, w_router)
    slot, src, tile_expert, n_used = _moe_plan(eid, rank, counts, bm, n_sorted // bm)
    xn_sorted = _moe_gather(xn_packed, src, bm)
    h = _grouped_mm("moe_up", xn_sorted, [_bf16(w_gate), _bf16(w_up)], tile_expert, n_used,
                    bm=bm, packed=True, swiglu=True, out_dtype=jnp.bfloat16)
    y = _grouped_mm("moe_down", h, [_bf16(w_down)], tile_expert, n_used,
                    bm=bm, packed=False, swiglu=False, out_dtype=jnp.float32)
    return _moe_combine(x2, y, slot, wts, g_final)


def _swiglu_up(name, xn, w_gate, w_up):
    m, d = xn.shape
    f = w_gate.shape[1]
    bm, bn = _tile(m, MM_BM), _tile(f, MM_BN_MULTI)
    return _mm(name, [(xn, _a_spec(bm, d))], [(w_gate, _w_spec(d, bn)), (w_up, _w_spec(d, bn))], [],
               [(0, 0), (0, 1)], _ep_swiglu, m=m, n=f, bm=bm, bn=bn, nk=1, out_dtype=jnp.bfloat16)


def _bf16(w):
    return w.astype(jnp.bfloat16)


def _mixer(x2, batch, seq, g_mix, w_in, b_forget, b_gate, g_q, g_kv, w_uq, w_ukv, sinks, w_branch,
           w_out, tables):
    d = x2.shape[1]
    fox_w = FOX_HEADS * FOX_HEAD_DIM
    swa_w = SWA_Q_HEADS * SWA_HEAD_DIM
    swa_kv_w = SWA_KV_HEADS * SWA_HEAD_DIM
    mla_w = MLA_HEADS * MLA_V_DIM
    o_lat = 0
    o_fox = Q_LORA + KV_LORA + ROPE_DIM
    o_flog = o_fox + 3 * fox_w
    o_swa = o_flog + FOX_HEADS
    o_gate = o_swa + swa_w + 2 * swa_kv_w

    lat_w = Q_LORA + KV_LORA + 2 * LANE
    lat_pad = -(lat_w) % 512
    w_lat = _bf16(jnp.concatenate([
        jnp.pad(w_in[:, o_lat:o_fox], ((0, 0), (0, LANE - ROPE_DIM))),
        jnp.pad(w_in[:, o_flog:o_swa], ((0, 0), (0, LANE - FOX_HEADS + lat_pad)))], axis=1))
    flog_col_block = (Q_LORA + KV_LORA + LANE) // LANE
    w_fox = _bf16(w_in[:, o_fox:o_flog])
    w_swa = _bf16(w_in[:, o_swa:o_gate])
    w_gate = _bf16(w_in[:, o_gate:])

    xn = _rmsnorm(x2, g_mix, jnp.bfloat16, name="norm_mix")
    lat = _linear("in_latent", xn, w_lat, jnp.float32, bn=512)
    fox_bn, swa_bn = _tile(3 * fox_w, MM_BN), _tile(swa_w + 2 * swa_kv_w, MM_BN)
    assert fox_w % fox_bn == 0 and swa_w % swa_bn == 0
    fqkv = _linear("in_fox", xn, w_fox, jnp.bfloat16, bn=fox_bn, epilogue=functools.partial(
        _ep_scale_leading, fox_w // fox_bn, FOX_HEAD_DIM ** -0.5 * LOG2E))
    sqkv = _linear("in_swa", xn, w_swa, jnp.bfloat16, bn=swa_bn, epilogue=functools.partial(
        _ep_scale_leading, swa_w // swa_bn, SWA_HEAD_DIM ** -0.5 * LOG2E))
    gates = _linear("in_gate", xn, w_gate, jnp.bfloat16, epilogue=_ep_sigmoid_bias,
                    extras=[(b_gate.reshape(1, -1).astype(jnp.float32), "row")])

    cq_n = _rmsnorm(lat, g_q, jnp.bfloat16, col_block=0, width=Q_LORA, name="norm_q")
    w_uq_p = w_uq.reshape(Q_LORA, MLA_HEADS, NOPE_DIM + ROPE_DIM)
    w_uq_p = _bf16(jnp.pad(w_uq_p, ((0, 0), (0, 0), (0, ROPE_DIM))).reshape(Q_LORA, MLA_HEADS * MLA_QK))
    w_ukv_h = w_ukv.reshape(KV_LORA, MLA_HEADS, NOPE_DIM + MLA_V_DIM)
    w_k = _bf16(w_ukv_h[:, :, :NOPE_DIM].reshape(KV_LORA, MLA_HEADS * NOPE_DIM))
    w_v = _bf16(w_ukv_h[:, :, NOPE_DIM:].reshape(KV_LORA, MLA_HEADS * MLA_V_DIM))
    q_mla = _mla_q(cq_n, w_uq_p, tables, seq)
    k_mla, v_mla = _mla_kv(lat, Q_LORA // KV_LORA, g_kv, w_k, w_v, (Q_LORA + KV_LORA) // LANE, tables, seq)
    o_mla = _flash("attn_mla", q_mla, k_mla, v_mla, q_col0=0, k_col0=0, v_col0=0, dk=MLA_QK,
                   dv=MLA_V_DIM, heads=MLA_HEADS, batch=batch, seq=seq)

    b_pad = jnp.pad(b_forget.astype(jnp.float32), (0, LANE - FOX_HEADS)).reshape(1, LANE)
    cum = _forget_cumsum(lat, flog_col_block, b_pad, batch, seq)
    cum_row = cum[:, :FOX_HEADS].reshape(batch, seq, FOX_HEADS).transpose(0, 2, 1)
    cum_row = cum_row.reshape(batch * FOX_HEADS, seq)
    o_foxa = _flash("attn_fox", fqkv, fqkv, fqkv, q_col0=0, k_col0=FOX_HEADS, v_col0=2 * FOX_HEADS,
                    dk=FOX_HEAD_DIM, dv=FOX_HEAD_DIM, heads=FOX_HEADS, batch=batch, seq=seq,
                    cum_col=cum, cum_row=cum_row)

    o_swa_a = _swa(sqkv, sinks, batch, seq)

    m = x2.shape[0]
    bm, bn = _tile(m, MM_BM), _tile(d, MM_BN_MULTI)
    nj = d // bn
    wb = [_bf16(w_branch[:mla_w]), _bf16(w_branch[mla_w:mla_w + fox_w]), _bf16(w_branch[mla_w + fox_w:])]
    outs = [o_mla, o_foxa, o_swa_a]
    merged = _mm("branch_merge",
                 [(o, _a_spec(bm, o.shape[1])) for o in outs],
                 [(w, _w_spec(w.shape[0], bn)) for w in wb],
                 [(gates, _tile_spec(bm, bn, col_off=b * nj)) for b in range(N_BRANCHES)],
                 [(0, 0), (1, 1), (2, 2)], _ep_merge,
                 m=m, n=d, bm=bm, bn=bn, nk=1, out_dtype=jnp.bfloat16)
    return _linear("out_proj", merged, _bf16(w_out), jnp.float32, epilogue=_ep_residual,
                   extras=[(x2, "tile")])


def kernel(x, g_mix_norm, w_in, b_forget, b_gate, g_q_norm, g_kv_norm, w_uq, w_ukv, sinks, w_branch,
           w_out, g_ffn_norm, w_dense_gate, w_dense_up, w_dense_down, w_router, w_exp_gate, w_exp_up,
           w_exp_down, g_final):
    batch, seq, d = x.shape
    depth = w_in.shape[0]
    x2 = x.reshape(batch * seq, d)
    tables = _rope_tables(seq)
    for l in range(depth):
        x2 = _mixer(x2, batch, seq, g_mix_norm[l], w_in[l], b_forget[l], b_gate[l], g_q_norm[l],
                    g_kv_norm[l], w_uq[l], w_ukv[l], sinks[l], w_branch[l], w_out[l], tables)
        j = l // 2
        last = l == depth - 1
        if l % 2 == 0:
            xn = _rmsnorm(x2, g_ffn_norm[l], jnp.bfloat16, name="norm_ffn")
            h = _swiglu_up("dense_up", xn, _bf16(w_dense_gate[j]), _bf16(w_dense_up[j]))
            x2 = _linear("ffn_down", h, _bf16(w_dense_down[j]), jnp.float32, epilogue=_ep_residual,
                         extras=[(x2, "tile")])
            if last:
                x2 = _rmsnorm(x2, g_final, x.dtype, name="norm_final")
        else:
            x2 = _moe_ffn(x2, g_ffn_norm[l], w_router[j], w_exp_gate[j], w_exp_up[j], w_exp_down[j],
                          g_final if last else None)
    return x2.reshape(batch, seq, d)
```
